```python
import jax
import jax.numpy as jnp
from jax import lax
import numpy as np

D_MODEL = 1024
BATCH = 16
SEQ = 2048
DEPTH = 2

D_MIX = D_MODEL
N_GROUPS = 4
GROUP_WIDTH = D_MIX // N_GROUPS
A_HEADS = 4
A_DK = GROUP_WIDTH // A_HEADS
A_DV = GROUP_WIDTH // A_HEADS
A_CHUNK = 16
LB_FLOOR = 1e-30
B_HEADS = 4
B_Q_LORA = 256
B_KV_LORA = 128
B_NOPE = 64
B_ROPE = 32
B_V = GROUP_WIDTH // B_HEADS
ROPE_THETA = 10000.0
C_HEADS = 4
C_HEAD_DIM = GROUP_WIDTH // C_HEADS
FOX_GATE_BIAS = 3.0
D_GROUPS = 4
D_GROUP_DIM = GROUP_WIDTH // D_GROUPS
D_CHUNK = 128
Q_BLOCK = 128
D_FF = 2816
N_MOD = 9
ALPHA = (2 * DEPTH) ** 0.25
BETA = (8 * DEPTH) ** -0.25
LN_EPS = 1e-5
RMS_EPS = 1e-6
MIX_SPLIT_SIZES = (GROUP_WIDTH, GROUP_WIDTH, GROUP_WIDTH, GROUP_WIDTH,
                   B_Q_LORA, B_KV_LORA, B_ROPE,
                   GROUP_WIDTH, GROUP_WIDTH, GROUP_WIDTH, C_HEADS,
                   GROUP_WIDTH, GROUP_WIDTH)
MIX_IN_COLS = sum(MIX_SPLIT_SIZES)

kernel_name = "hybrid_hgrn2_mla_fox_gmlp_deepnorm_block"


def layer_norm(x, g, b):
    xf = x.astype(jnp.float32)
    mu = jnp.mean(xf, axis=-1, keepdims=True)
    var = jnp.mean(jnp.square(xf - mu), axis=-1, keepdims=True)
    return ((xf - mu) * lax.rsqrt(var + LN_EPS)).astype(x.dtype) * g + b


def rms_norm(x, g):
    xf = x.astype(jnp.float32)
    return (xf * lax.rsqrt(jnp.mean(xf * xf, axis=-1, keepdims=True) + RMS_EPS)).astype(x.dtype) * g


def swiglu_ffn(h, w_in, w_out):
    gate, up = jnp.split(h @ w_in, 2, axis=-1)
    return (jax.nn.silu(gate) * up) @ w_out


def rope(x, pos):
    half = x.shape[-1] // 2
    inv_freq = ROPE_THETA ** (-jnp.arange(half, dtype=jnp.float32) / half)
    ang = pos.astype(jnp.float32)[:, None] * inv_freq[None, :]
    cos = jnp.cos(ang)[None, :, None, :].astype(x.dtype)
    sin = jnp.sin(ang)[None, :, None, :].astype(x.dtype)
    x1, x2 = x[..., :half], x[..., half:]
    return jnp.concatenate([x1 * cos - x2 * sin, x1 * sin + x2 * cos], axis=-1)


def causal_softmax_attention(q, k, v, cum_log_f=None):
    b, s, h, dk = q.shape
    dv = v.shape[-1]
    n_blocks = s // Q_BLOCK
    scale = dk ** -0.5
    k_pos = jnp.arange(s)
    cum_t = None if cum_log_f is None else jnp.swapaxes(cum_log_f, 1, 2)

    def one_block(i):
        start = i * Q_BLOCK
        q_i = lax.dynamic_slice_in_dim(q, start, Q_BLOCK, axis=1)
        logits = jnp.einsum('bqhd,bkhd->bhqk', q_i, k,
                            preferred_element_type=jnp.float32) * scale
        if cum_t is not None:
            f_i = lax.dynamic_slice_in_dim(cum_t, start, Q_BLOCK, axis=2)
            logits = logits + (f_i[..., :, None] - cum_t[..., None, :])
        q_pos = start + jnp.arange(Q_BLOCK)
        logits = jnp.where(k_pos[None, :] <= q_pos[:, None], logits, -jnp.inf)
        p = jax.nn.softmax(logits, axis=-1).astype(v.dtype)
        return jnp.einsum('bhqk,bkhd->bqhd', p, v)

    out = lax.map(one_block, jnp.arange(n_blocks))
    return jnp.moveaxis(out, 0, 1).reshape(b, s, h * dv)


def hgrn2_mixer(q, f_logit, inp, g_out, lb, norm_g):
    b, s, _ = q.shape
    dt = q.dtype
    n_chunks = s // A_CHUNK
    lbf = lb.astype(jnp.float32)
    log_f = jnp.logaddexp(jnp.log(jnp.maximum(lbf, LB_FLOOR)),
                          jnp.log1p(-lbf) + jax.nn.log_sigmoid(f_logit.astype(jnp.float32)))
    k = -jnp.expm1(log_f)
    qf = jax.nn.silu(q.astype(jnp.float32))
    shp_k = (b, n_chunks, A_CHUNK, A_HEADS, A_DK)
    qf, k, log_f = qf.reshape(shp_k), k.reshape(shp_k), log_f.reshape(shp_k)
    v = inp.astype(jnp.float32).reshape(b, n_chunks, A_CHUNK, A_HEADS, A_DV)
    G = jnp.cumsum(log_f, axis=2)
    G_last = G[:, :, -1:]
    causal = jnp.tril(jnp.ones((A_CHUNK, A_CHUNK), dtype=bool))[None, None, :, :, None, None]
    rel = jnp.where(causal, G[:, :, :, None] - G[:, :, None, :], -jnp.inf)
    decay = jnp.exp(rel)
    scores = jnp.einsum('bctha,bcsha,bctsha->bchts', qf, k, decay)
    o_intra = jnp.einsum('bchts,bcshv->bcthv', scores, v)
    q_dec = qf * jnp.exp(G)
    k_to_end = k * jnp.exp(G_last - G)
    chunk_kv = jnp.einsum('bcsha,bcshv->cbhav', k_to_end, v)
    chunk_decay = jnp.transpose(jnp.exp(G_last[:, :, 0]), (1, 0, 2, 3))

    def step(state, xs):
        dec, kv = xs
        return dec[..., None] * state + kv, state

    state0 = jnp.zeros((b, A_HEADS, A_DK, A_DV), jnp.float32)
    _, state_in = lax.scan(step, state0, (chunk_decay, chunk_kv))
    o_inter = jnp.einsum('bctha,cbhav->bcthv', q_dec, state_in)
    o = (o_intra + o_inter).reshape(b, s, A_HEADS, A_DV)
    o = rms_norm(o, norm_g.astype(jnp.float32).reshape(A_HEADS, A_DV)).reshape(b, s, GROUP_WIDTH)
    return (o * jax.nn.silu(g_out.astype(jnp.float32))).astype(dt)


def mla_mixer(c_q, c_kv, k_rope, q_norm_g, kv_norm_g, w_uq, w_ukv, pos):
    b, s, _ = c_q.shape
    q = (rms_norm(c_q, q_norm_g) @ w_uq).reshape(b, s, B_HEADS, B_NOPE + B_ROPE)
    q_nope, q_rope = q[..., :B_NOPE], q[..., B_NOPE:]
    kv = (rms_norm(c_kv, kv_norm_g) @ w_ukv).reshape(b, s, B_HEADS, B_NOPE + B_V)
    k_nope, v = kv[..., :B_NOPE], kv[..., B_NOPE:]
    k_r = rope(k_rope[:, :, None, :], pos)
    q = jnp.concatenate([q_nope, rope(q_rope, pos)], axis=-1)
    k = jnp.concatenate([k_nope, jnp.broadcast_to(k_r, (b, s, B_HEADS, B_ROPE))], axis=-1)
    return causal_softmax_attention(q, k, v)


def fox_mixer(q, k, v, f_logit, b_f):
    b, s, _ = q.shape
    shp = (b, s, C_HEADS, C_HEAD_DIM)
    log_f = jax.nn.log_sigmoid(f_logit.astype(jnp.float32) + b_f.astype(jnp.float32))
    cum = jnp.cumsum(log_f, axis=1)
    return causal_softmax_attention(q.reshape(shp), k.reshape(shp), v.reshape(shp), cum)


def gmlp_mixer(u, v, ln_g, ln_b, w_s, b_s):
    b, s, _ = u.shape
    n_chunks = s // D_CHUNK
    u = jax.nn.gelu(u)
    v = layer_norm(jax.nn.gelu(v), ln_g, ln_b).reshape(b, n_chunks, D_CHUNK, D_GROUPS, D_GROUP_DIM)
    causal = jnp.tril(jnp.ones((D_CHUNK, D_CHUNK), dtype=bool))
    w = jnp.where(causal, w_s, 0.0)
    mixed = jnp.einsum('gts,bcsgd->bctgd', w, v) + jnp.swapaxes(b_s, 0, 1)[:, :, None]
    return u * mixed.reshape(b, s, GROUP_WIDTH)


def hybrid_token_mixer(h, w_in, w_out, lb, hgrn_norm_g, mla_q_norm_g, mla_kv_norm_g,
                       mla_w_uq, mla_w_ukv, fox_b_f, gmlp_ln_g, gmlp_ln_b, gmlp_w_s, gmlp_b_s):
    split_idx = [int(i) for i in np.cumsum(MIX_SPLIT_SIZES)[:-1]]
    proj = h @ w_in
    (a_q, a_f, a_i, a_g, b_cq, b_ckv, b_kr,
     c_q, c_k, c_v, c_f, d_u, d_v) = jnp.split(proj, split_idx, axis=-1)
    pos = jnp.arange(h.shape[1])
    o_a = hgrn2_mixer(a_q, a_f, a_i, a_g, lb, hgrn_norm_g)
    o_b = mla_mixer(b_cq, b_ckv, b_kr, mla_q_norm_g, mla_kv_norm_g, mla_w_uq, mla_w_ukv, pos)
    o_c = fox_mixer(c_q, c_k, c_v, c_f, fox_b_f)
    o_d = gmlp_mixer(d_u, d_v, gmlp_ln_g, gmlp_ln_b, gmlp_w_s, gmlp_b_s)
    return jnp.concatenate([o_a, o_b.astype(h.dtype), o_c.astype(h.dtype), o_d], axis=-1) @ w_out


def setup_inputs(seed: int = 0) -> dict:
    key = jax.random.key(seed)
    ks = jax.random.split(key, 23)
    L = DEPTH

    def nrm(k, shape, scale):
        return scale * jax.random.normal(k, shape, jnp.float32)

    return {
        'x': nrm(ks[0], (BATCH, SEQ, D_MODEL), 1.0),
        'c': nrm(ks[1], (BATCH, D_MODEL), 1.0),
        'ada_w': nrm(ks[2], (L, D_MODEL, N_MOD * D_MODEL), 0.1 * D_MODEL ** -0.5),
        'ada_b': nrm(ks[3], (L, N_MOD * D_MODEL), 0.01),
        'ln_g': 1.0 + nrm(ks[4], (L, 3, D_MODEL), 0.02),
        'ln_b': nrm(ks[5], (L, 3, D_MODEL), 0.02),
        'ffn1_w_in': nrm(ks[6], (L, D_MODEL, 2 * D_FF), D_MODEL ** -0.5),
        'ffn1_w_out': nrm(ks[7], (L, D_FF, D_MODEL), BETA * D_FF ** -0.5),
        'ffn2_w_in': nrm(ks[8], (L, D_MODEL, 2 * D_FF), D_MODEL ** -0.5),
        'ffn2_w_out': nrm(ks[9], (L, D_FF, D_MODEL), BETA * D_FF ** -0.5),
        'mix_w_in': nrm(ks[10], (L, D_MODEL, MIX_IN_COLS), D_MODEL ** -0.5),
        'mix_w_out': nrm(ks[11], (L, D_MIX, D_MODEL), BETA * D_MIX ** -0.5),
        'hgrn_lb_logits': nrm(ks[12], (L, GROUP_WIDTH), 0.5),
        'hgrn_norm_g': 1.0 + nrm(ks[13], (L, GROUP_WIDTH), 0.02),
        'mla_q_norm_g': 1.0 + nrm(ks[14], (L, B_Q_LORA), 0.02),
        'mla_kv_norm_g': 1.0 + nrm(ks[15], (L, B_KV_LORA), 0.02),
        'mla_w_uq': nrm(ks[16], (L, B_Q_LORA, B_HEADS * (B_NOPE + B_ROPE)), B_Q_LORA ** -0.5),
        'mla_w_ukv': nrm(ks[17], (L, B_KV_LORA, B_HEADS * (B_NOPE + B_V)), B_KV_LORA ** -0.5),
        'fox_b_f': FOX_GATE_BIAS + nrm(ks[18], (L, C_HEADS), 0.5),
        'gmlp_ln_g': 1.0 + nrm(ks[19], (L, GROUP_WIDTH), 0.02),
        'gmlp_ln_b': nrm(ks[20], (L, GROUP_WIDTH), 0.02),
        'gmlp_w_s': nrm(ks[21], (L, D_GROUPS, D_CHUNK, D_CHUNK), 0.5 * D_CHUNK ** -0.5),
        'gmlp_b_s': 1.0 + nrm(ks[22], (L, D_GROUPS, D_CHUNK), 0.02),
    }


def reference(x, c, ada_w, ada_b, ln_g, ln_b, ffn1_w_in, ffn1_w_out, ffn2_w_in, ffn2_w_out,
              mix_w_in, mix_w_out, hgrn_lb_logits, hgrn_norm_g, mla_q_norm_g, mla_kv_norm_g,
              mla_w_uq, mla_w_ukv, fox_b_f, gmlp_ln_g, gmlp_ln_b, gmlp_w_s, gmlp_b_s):
    lb_sm = jax.nn.softmax(hgrn_lb_logits.astype(jnp.float32), axis=0)
    lb_all = jnp.cumsum(lb_sm, axis=0) - lb_sm[0]
    c_act = jax.nn.silu(c)
    for l in range(DEPTH):
        mod = (c_act @ ada_w[l] + ada_b[l])[:, None, :]
        sh1, sc1, g1, sh2, sc2, g2, sh3, sc3, g3 = jnp.split(mod, N_MOD, axis=-1)
        h = x * (1.0 + sc1) + sh1
        x = layer_norm(ALPHA * x + 0.5 * (1.0 + g1) * swiglu_ffn(h, ffn1_w_in[l], ffn1_w_out[l]),
                       ln_g[l, 0], ln_b[l, 0])
        h = x * (1.0 + sc2) + sh2
        mixed = hybrid_token_mixer(h, mix_w_in[l], mix_w_out[l], lb_all[l], hgrn_norm_g[l],
                                   mla_q_norm_g[l], mla_kv_norm_g[l], mla_w_uq[l], mla_w_ukv[l],
                                   fox_b_f[l], gmlp_ln_g[l], gmlp_ln_b[l], gmlp_w_s[l], gmlp_b_s[l])
        x = layer_norm(ALPHA * x + (1.0 + g2) * mixed, ln_g[l, 1], ln_b[l, 1])
        h = x * (1.0 + sc3) + sh3
        x = layer_norm(ALPHA * x + 0.5 * (1.0 + g3) * swiglu_ffn(h, ffn2_w_in[l], ffn2_w_out[l]),
                       ln_g[l, 2], ln_b[l, 2])
    return x
```

```python
import functools

import jax
import jax.numpy as jnp
from jax import lax
from jax.experimental import pallas as pl
from jax.experimental.pallas import tpu as pltpu

BF = jnp.bfloat16
F32 = jnp.float32

N_HEADS = 4
GROUP_WIDTH = 256
HEAD_DIM = 64
HEAD_SHIFT = 6
HGRN_CHUNK = 16
CHUNK_SHIFT = 4
MLA_Q_LORA = 256
MLA_KV_LORA = 128
MLA_NOPE = 64
MLA_ROPE = 32
ROPE_THETA = 10000.0
GMLP_CHUNK = 128
N_MOD = 9
LN_EPS = 1e-5
RMS_EPS = 1e-6
LB_FLOOR = 1e-30
NEG_BIG = -1e30

LANES = 128
HEAD_SLOT = 128
VMEM_LIMIT_BYTES = 56 * 1024 * 1024

FFN_ROWS = 512
FFN_CHUNK = 256
MIX_ROWS = 256
HGRN_ROWS = 128
ATT_Q_ROWS = 256
ATT_K_ROWS = 256
OUT_ROWS = 512


def _dot(a, b):
    return jnp.dot(a, b, preferred_element_type=F32)


def _dot_nt(a, b):
    return lax.dot_general(a, b, (((1,), (1,)), ((), ())), preferred_element_type=F32)


def _sigmoid(x):
    return jax.nn.sigmoid(x)


def _silu(x):
    return x * _sigmoid(x)


def _log_sigmoid(x):
    return jnp.minimum(x, 0.0) - jnp.log1p(jnp.exp(-jnp.abs(x)))


def _logaddexp(a, b):
    return jnp.maximum(a, b) + jnp.log1p(jnp.exp(-jnp.abs(a - b)))


def _gelu_tanh(x):
    c = 0.7978845608028654
    return 0.5 * x * (1.0 + jnp.tanh(c * (x + 0.044715 * (x * x * x))))


def _layer_norm(r, g, b):
    mu = jnp.mean(r, axis=-1, keepdims=True)
    d = r - mu
    var = jnp.mean(d * d, axis=-1, keepdims=True)
    return d * lax.rsqrt(var + LN_EPS) * g + b


def _rms_norm(r, g):
    return r * lax.rsqrt(jnp.mean(r * r, axis=-1, keepdims=True) + RMS_EPS) * g


def _split3(x):
    hi = x.astype(BF)
    r1 = x - hi.astype(F32)
    mid = r1.astype(BF)
    lo = (r1 - mid.astype(F32)).astype(BF)
    return hi, mid, lo


def _dot_exact_lhs(m_bf, x):
    hi, mid, lo = _split3(x)
    return _dot(m_bf, hi) + _dot(m_bf, mid) + _dot(m_bf, lo)


def _iota2(shape, dim):
    return lax.broadcasted_iota(jnp.int32, shape, dim)


def _mod_kernel(c_ref, w_ref, b_ref, o_ref):
    c = c_ref[...]
    o_ref[...] = _dot(_silu(c).astype(BF), w_ref[...].astype(BF)) + b_ref[...]


def _modulation(c, ada_w, ada_b):
    depth, d, n = ada_w.shape
    b = c.shape[0]
    tn = 1152 if n % 1152 == 0 else n
    return pl.pallas_call(
        _mod_kernel,
        grid=(depth, n // tn),
        in_specs=[
            pl.BlockSpec((b, d), lambda l, j: (0, 0)),
            pl.BlockSpec((None, d, tn), lambda l, j: (l, 0, j)),
            pl.BlockSpec((None, 1, tn), lambda l, j: (l, 0, j)),
        ],
        out_specs=pl.BlockSpec((None, b, tn), lambda l, j: (l, 0, j)),
        out_shape=jax.ShapeDtypeStruct((depth, b, n), F32),
        compiler_params=pltpu.CompilerParams(
            dimension_semantics=("arbitrary", "arbitrary"), vmem_limit_bytes=VMEM_LIMIT_BYTES),
        name="adaln_modulation",
    )(c, ada_w, ada_b.reshape(depth, 1, n))


def _ffn_kernel(x_ref, mod_ref, wg_ref, wu_ref, wo_ref, lng_ref, lnb_ref, o_ref, acc_ref,
                *, mod_row, n_chunks, alpha):
    x = x_ref[...]
    sh = mod_ref[mod_row:mod_row + 1, :]
    sc = mod_ref[mod_row + 1:mod_row + 2, :]
    gt = mod_ref[mod_row + 2:mod_row + 3, :]
    h = (x * (1.0 + sc) + sh).astype(BF)
    for c in range(n_chunks):
        gate = _dot(h, wg_ref[c])
        up = _dot(h, wu_ref[c])
        a = (_silu(gate) * up).astype(BF)
        y = _dot(a, wo_ref[c])
        if c == 0:
            acc_ref[...] = y
        else:
            acc_ref[...] += y
    r = alpha * x + (0.5 * (1.0 + gt)) * acc_ref[...]
    o_ref[...] = _layer_norm(r, lng_ref[...], lnb_ref[...])


def _ffn_sublayer(x, mod, w_in, w_out, ln_g, ln_b, *, mod_row, alpha):
    b, s, d = x.shape
    dff = w_out.shape[0]
    ck = FFN_CHUNK if dff % FFN_CHUNK == 0 else dff
    n_chunks = dff // ck
    tm = min(FFN_ROWS, s)
    w_in = w_in.astype(BF)
    wg = w_in[:, :dff].reshape(d, n_chunks, ck).transpose(1, 0, 2)
    wu = w_in[:, dff:].reshape(d, n_chunks, ck).transpose(1, 0, 2)
    wo = w_out.astype(BF).reshape(n_chunks, ck, d)
    const3 = lambda bi, i: (0, 0, 0)
    return pl.pallas_call(
        functools.partial(_ffn_kernel, mod_row=mod_row, n_chunks=n_chunks, alpha=alpha),
        grid=(b, s // tm),
        in_specs=[
            pl.BlockSpec((None, tm, d), lambda bi, i: (bi, i, 0)),
            pl.BlockSpec((None, N_MOD, d), lambda bi, i: (bi, 0, 0)),
            pl.BlockSpec((n_chunks, d, ck), const3),
            pl.BlockSpec((n_chunks, d, ck), const3),
            pl.BlockSpec((n_chunks, ck, d), const3),
            pl.BlockSpec((1, d), lambda bi, i: (0, 0)),
            pl.BlockSpec((1, d), lambda bi, i: (0, 0)),
        ],
        out_specs=pl.BlockSpec((None, tm, d), lambda bi, i: (bi, i, 0)),
        out_shape=jax.ShapeDtypeStruct((b, s, d), F32),
        scratch_shapes=[pltpu.VMEM((tm, d), F32)],
        compiler_params=pltpu.CompilerParams(
            dimension_semantics=("arbitrary", "arbitrary"), vmem_limit_bytes=VMEM_LIMIT_BYTES),
        name="ffn_sublayer",
    )(x, mod, wg, wu, wo, ln_g.reshape(1, d), ln_b.reshape(1, d))


_C_HG = 0
_C_CQ = 1024
_C_CKV = 1280
_C_KR = 1408
_C_KRS = 1536
_C_FQ = 1664
_C_FK = 2176
_C_FV = 2688
_C_FF = 3200
_C_DU = 3328
_C_DV = 3584
_C_END = 3840


def _mix_in_kernel(x_ref, mod_ref, w_ref, wuq_ref, wukv_ref, qg_ref, kvg_ref, ct_ref, st_ref,
                   fb_ref, dlg_ref, dlb_ref, ws_ref, bs_ref,
                   hg_ref, mq_ref, mk_ref, mv_ref, fq_ref, fk_ref, fv_ref, ff_ref, od_ref,
                   carry_ref, *, tm, mla_scale, fox_scale):
    x = x_ref[...]
    sh = mod_ref[3:4, :]
    sc = mod_ref[4:5, :]
    h = (x * (1.0 + sc) + sh).astype(BF)

    def proj(lo, hi):
        return _dot(h, w_ref[:, lo:hi])

    hg_ref[...] = proj(_C_HG, _C_CQ)

    ct = ct_ref[...]
    st = st_ref[...]
    ct4 = jnp.concatenate([ct] * N_HEADS, axis=1)
    st4 = jnp.concatenate([st] * N_HEADS, axis=1)
    nq = _rms_norm(proj(_C_CQ, _C_CKV), qg_ref[...]).astype(BF)
    q2 = _dot(nq, wuq_ref[...])
    nslot = N_HEADS * HEAD_SLOT
    q_rot = q2[:, :nslot] * ct4 + q2[:, nslot:] * st4
    mq_ref[...] = (q_rot * mla_scale).astype(BF)
    nkv = _rms_norm(proj(_C_CKV, _C_KR), kvg_ref[...]).astype(BF)
    kv2 = _dot(nkv, wukv_ref[...])
    kr_rot = proj(_C_KR, _C_KRS) * ct + proj(_C_KRS, _C_FQ) * st
    mk_ref[...] = (kv2[:, :nslot] + jnp.concatenate([kr_rot] * N_HEADS, axis=1)).astype(BF)
    mv_ref[...] = kv2[:, nslot:].astype(BF)

    fq_ref[...] = (proj(_C_FQ, _C_FK) * fox_scale).astype(BF)
    fk_ref[...] = proj(_C_FK, _C_FV).astype(BF)
    fv_ref[...] = proj(_C_FV, _C_FF).astype(BF)
    logf = _log_sigmoid(proj(_C_FF, _C_DU) + fb_ref[...])
    tri = (_iota2((tm, tm), 0) >= _iota2((tm, tm), 1)).astype(BF)

    @pl.when(pl.program_id(1) == 0)
    def _():
        carry_ref[...] = jnp.zeros_like(carry_ref)

    cum = _dot_exact_lhs(tri, logf) + carry_ref[...]
    ff_ref[...] = cum
    carry_ref[...] = cum[tm - 1:tm, :]

    u = _gelu_tanh(proj(_C_DU, _C_DV))
    vn = _layer_norm(_gelu_tanh(proj(_C_DV, _C_END)), dlg_ref[...], dlb_ref[...])
    cc = GMLP_CHUNK
    causal = _iota2((cc, cc), 0) >= _iota2((cc, cc), 1)
    lane_group = _iota2((cc, GROUP_WIDTH), 1) >> HEAD_SHIFT
    for j in range(tm // cc):
        vj = vn[j * cc:(j + 1) * cc, :]
        mixed = bs_ref[...]
        for g in range(N_HEADS):
            wg = jnp.where(causal, ws_ref[g], 0.0).astype(BF)
            mixed = mixed + _dot(wg, jnp.where(lane_group == g, vj, 0.0).astype(BF))
        od_ref[j * cc:(j + 1) * cc, :] = (u[j * cc:(j + 1) * cc, :] * mixed).astype(BF)


def _place_heads(w, width, offsets):
    k = w.shape[0]
    cols = []
    for hd in range(N_HEADS):
        blk = w[:, hd * width:(hd + 1) * width]
        off = offsets[hd]
        cols.append(jnp.pad(blk, ((0, 0), (off, HEAD_SLOT - off - width))))
    return jnp.concatenate(cols, axis=1).reshape(k, N_HEADS * HEAD_SLOT)


def _mix_in_weights(mix_w_in, mla_w_uq, mla_w_ukv):
    d = mix_w_in.shape[0]
    gw = GROUP_WIDTH
    sizes = (gw, gw, gw, gw, MLA_Q_LORA, MLA_KV_LORA, MLA_ROPE, gw, gw, gw, N_HEADS, gw, gw)
    parts, o = [], 0
    for sz in sizes:
        parts.append(mix_w_in[:, o:o + sz])
        o += sz
    (a_q, a_f, a_i, a_g, b_cq, b_ckv, b_kr, c_q, c_k, c_v, c_f, d_u, d_v) = parts
    half = MLA_ROPE // 2
    z = lambda n: jnp.zeros((d, n), mix_w_in.dtype)
    kr = jnp.concatenate([z(MLA_NOPE), b_kr, z(HEAD_SLOT - MLA_NOPE - MLA_ROPE)], axis=1)
    krs = jnp.concatenate([z(MLA_NOPE), b_kr[:, half:], b_kr[:, :half],
                           z(HEAD_SLOT - MLA_NOPE - MLA_ROPE)], axis=1)
    zero_off = (0,) * N_HEADS
    pair_off = tuple((hd % 2) * HEAD_DIM for hd in range(N_HEADS))
    w_big = jnp.concatenate([
        a_q, a_f, a_i, a_g, b_cq, b_ckv, kr, krs,
        _place_heads(c_q, HEAD_DIM, zero_off), _place_heads(c_k, HEAD_DIM, zero_off),
        _place_heads(c_v, HEAD_DIM, pair_off),
        jnp.pad(c_f, ((0, 0), (0, LANES - N_HEADS))), d_u, d_v], axis=1).astype(BF)
    assert w_big.shape[1] == _C_END
    qk = MLA_NOPE + MLA_ROPE
    uq = mla_w_uq.reshape(MLA_Q_LORA, N_HEADS, qk)
    zq = lambda n: jnp.zeros((MLA_Q_LORA, N_HEADS, n), mla_w_uq.dtype)
    uq_n = jnp.concatenate([uq, zq(HEAD_SLOT - qk)], axis=2)
    uq_s = jnp.concatenate([zq(MLA_NOPE), uq[:, :, MLA_NOPE + half:], uq[:, :, MLA_NOPE:MLA_NOPE + half],
                            zq(HEAD_SLOT - qk)], axis=2)
    w_uq2 = jnp.concatenate([uq_n.reshape(MLA_Q_LORA, -1), uq_s.reshape(MLA_Q_LORA, -1)], axis=1).astype(BF)
    ukv = mla_w_ukv.reshape(MLA_KV_LORA, N_HEADS, MLA_NOPE + HEAD_DIM)
    k_n = _place_heads(ukv[:, :, :MLA_NOPE].reshape(MLA_KV_LORA, -1), MLA_NOPE, zero_off)
    v_p = _place_heads(ukv[:, :, MLA_NOPE:].reshape(MLA_KV_LORA, -1), HEAD_DIM, pair_off)
    w_ukv2 = jnp.concatenate([k_n, v_p], axis=1).astype(BF)
    return w_big, w_uq2, w_ukv2


def _rope_tables(s):
    half = MLA_ROPE // 2
    inv_freq = ROPE_THETA ** (-jnp.arange(half, dtype=F32) / half)
    ang = jnp.arange(s, dtype=F32)[:, None] * inv_freq[None, :]
    cos, sin = jnp.cos(ang), jnp.sin(ang)
    tail = HEAD_SLOT - MLA_NOPE - MLA_ROPE
    ct = jnp.concatenate([jnp.ones((s, MLA_NOPE), F32), cos, cos, jnp.zeros((s, tail), F32)], axis=1)
    st = jnp.concatenate([jnp.zeros((s, MLA_NOPE), F32), -sin, sin, jnp.zeros((s, tail), F32)], axis=1)
    return ct, st


def _mix_in(x, mod, w_big, w_uq2, w_ukv2, q_norm_g, kv_norm_g, ct, st, fox_b_f,
            gmlp_ln_g, gmlp_ln_b, gmlp_w_s, gmlp_b_s):
    b, s, d = x.shape
    tm = min(MIX_ROWS, s)
    nslot = N_HEADS * HEAD_SLOT
    gw = GROUP_WIDTH
    fb = jnp.pad(fox_b_f.reshape(1, N_HEADS), ((0, 0), (0, LANES - N_HEADS)))
    bias_tile = jnp.repeat(gmlp_b_s.T, HEAD_DIM, axis=1)
    full2 = lambda bi, i: (0, 0)
    tok = lambda n: pl.BlockSpec((None, tm, n), lambda bi, i: (bi, i, 0))
    shp = lambda n, dt: jax.ShapeDtypeStruct((b, s, n), dt)
    return pl.pallas_call(
        functools.partial(_mix_in_kernel, tm=tm, mla_scale=float((MLA_NOPE + MLA_ROPE) ** -0.5),
                          fox_scale=float(HEAD_DIM ** -0.5)),
        grid=(b, s // tm),
        in_specs=[
            tok(d),
            pl.BlockSpec((None, N_MOD, d), lambda bi, i: (bi, 0, 0)),
            pl.BlockSpec(w_big.shape, full2),
            pl.BlockSpec(w_uq2.shape, full2),
            pl.BlockSpec(w_ukv2.shape, full2),
            pl.BlockSpec((1, MLA_Q_LORA), full2),
            pl.BlockSpec((1, MLA_KV_LORA), full2),
            pl.BlockSpec((tm, HEAD_SLOT), lambda bi, i: (i, 0)),
            pl.BlockSpec((tm, HEAD_SLOT), lambda bi, i: (i, 0)),
            pl.BlockSpec((1, LANES), full2),
            pl.BlockSpec((1, gw), full2),
            pl.BlockSpec((1, gw), full2),
            pl.BlockSpec(gmlp_w_s.shape, lambda bi, i: (0, 0, 0)),
            pl.BlockSpec(bias_tile.shape, full2),
        ],
        out_specs=[tok(4 * gw), tok(nslot), tok(nslot), tok(nslot), tok(nslot), tok(nslot), tok(nslot),
                   tok(LANES), tok(gw)],
        out_shape=[shp(4 * gw, F32), shp(nslot, BF), shp(nslot, BF), shp(nslot, BF),
                   shp(nslot, BF), shp(nslot, BF), shp(nslot, BF), shp(LANES, F32), shp(gw, BF)],
        scratch_shapes=[pltpu.VMEM((1, LANES), F32)],
        compiler_params=pltpu.CompilerParams(
            dimension_semantics=("arbitrary", "arbitrary"), vmem_limit_bytes=VMEM_LIMIT_BYTES),
        name="mixer_in",
    )(x, mod, w_big, w_uq2, w_ukv2, q_norm_g.reshape(1, -1), kv_norm_g.reshape(1, -1), ct, st, fb,
      gmlp_ln_g.reshape(1, gw), gmlp_ln_b.reshape(1, gw), gmlp_w_s, bias_tile)


def _hgrn_kernel(hg_ref, lbl_ref, ng_ref, o_ref, st_ref, p_ref, oacc_ref, *, layer, depth, tb):
    gw = GROUP_WIDTH
    ch = HGRN_CHUNK
    n_ch = tb // ch

    @pl.when(pl.program_id(1) == 0)
    def _():
        st_ref[...] = jnp.zeros_like(st_ref)

    rows = [lbl_ref[j:j + 1, :] for j in range(depth)]
    top = functools.reduce(jnp.maximum, rows)
    ex = [jnp.exp(r - top) for r in rows]
    sm = [e / sum(ex) for e in ex]
    lb = sum(sm[:layer + 1]) - sm[0]
    lb_floor = jnp.maximum(lb, LB_FLOOR)

    z_q = hg_ref[:, 0:gw]
    z_f = hg_ref[:, gw:2 * gw]
    val = hg_ref[:, 2 * gw:3 * gw]
    z_g = hg_ref[:, 3 * gw:4 * gw]
    log_f = _logaddexp(jnp.log(lb_floor), jnp.log1p(-lb) + _log_sigmoid(z_f))
    kk = (1.0 - lb) * _sigmoid(-z_f) - (lb_floor - lb)
    qf = _silu(z_q)

    r_i = _iota2((tb, tb), 0)
    c_i = _iota2((tb, tb), 1)
    same_chunk = (r_i >> CHUNK_SHIFT) == (c_i >> CHUNK_SHIFT)
    tri_blk = (same_chunk & (r_i >= c_i)).astype(BF)
    ones_blk = same_chunk.astype(BF)
    hi, mid, lo = _split3(log_f)
    g_cum = _dot(tri_blk, hi) + _dot(tri_blk, mid) + _dot(tri_blk, lo)
    g_last = _dot(ones_blk, hi) + _dot(ones_blk, mid) + _dot(ones_blk, lo)
    q_dec = (qf * jnp.exp(g_cum)).astype(BF)
    k_end = kk * jnp.exp(g_last - g_cum)
    dec = jnp.exp(g_last)

    head_r = _iota2((gw, gw), 0) >> HEAD_SHIFT
    head_c = _iota2((gw, gw), 1) >> HEAD_SHIFT
    same_head = head_r == head_c
    head_ones = same_head.astype(BF)

    val_t = val.T.astype(BF)
    row_chunk = _iota2((tb, gw), 0) >> CHUNK_SHIFT
    t_loc = _iota2((ch, gw), 0)

    for c in range(n_ch):
        r0 = c * ch
        g_c = g_cum[r0:r0 + ch, :]
        q_c = qf[r0:r0 + ch, :]
        k_c = kk[r0:r0 + ch, :]
        v_c = val[r0:r0 + ch, :]
        for s_ in range(ch):
            e_ts = jnp.where(t_loc >= s_, jnp.exp(g_c - g_c[s_:s_ + 1, :]), 0.0)
            p_ref[s_ * ch:(s_ + 1) * ch, :] = (e_ts * q_c * k_c[s_:s_ + 1, :]).astype(BF)
        scores = _dot(p_ref[...], head_ones)
        o_c = _dot_nt(q_dec[r0:r0 + ch, :], st_ref[...].astype(BF))
        for s_ in range(ch):
            o_c = o_c + scores[s_ * ch:(s_ + 1) * ch, :] * v_c[s_:s_ + 1, :]
        oacc_ref[r0:r0 + ch, :] = o_c
        k_m = jnp.where(row_chunk == c, k_end, 0.0).astype(BF)
        kv_t = jnp.where(same_head, _dot(val_t, k_m), 0.0)
        st_ref[...] = st_ref[...] * dec[r0:r0 + 1, :] + kv_t

    o = oacc_ref[...]
    h2, l2, _ = _split3(o * o)
    ms = (_dot(h2, head_ones) + _dot(l2, head_ones)) * (1.0 / HEAD_DIM)
    o = o * lax.rsqrt(ms + RMS_EPS) * ng_ref[...]
    o_ref[...] = (o * _silu(z_g)).astype(BF)


def _hgrn(hg, lb_logits, norm_g, *, layer):
    b, s, _ = hg.shape
    depth = lb_logits.shape[0]
    gw = GROUP_WIDTH
    tb = min(HGRN_ROWS, s)
    return pl.pallas_call(
        functools.partial(_hgrn_kernel, layer=layer, depth=depth, tb=tb),
        grid=(b, s // tb),
        in_specs=[
            pl.BlockSpec((None, tb, 4 * gw), lambda bi, i: (bi, i, 0)),
            pl.BlockSpec((depth, gw), lambda bi, i: (0, 0)),
            pl.BlockSpec((1, gw), lambda bi, i: (0, 0)),
        ],
        out_specs=pl.BlockSpec((None, tb, gw), lambda bi, i: (bi, i, 0)),
        out_shape=jax.ShapeDtypeStruct((b, s, gw), BF),
        scratch_shapes=[pltpu.VMEM((gw, gw), F32),
                        pltpu.VMEM((HGRN_CHUNK * HGRN_CHUNK, gw), BF),
                        pltpu.VMEM((tb, gw), F32)],
        compiler_params=pltpu.CompilerParams(
            dimension_semantics=("arbitrary", "arbitrary"), vmem_limit_bytes=VMEM_LIMIT_BYTES),
        name="hgrn2",
    )(hg, lb_logits, norm_g.reshape(1, gw))


def _attn_kernel(*refs, tq, tk, has_bias):
    if has_bias:
        q_ref, k_ref, v_ref, bq_ref, bk_ref, o_ref, m_ref, l_ref, acc_ref = refs
    else:
        q_ref, k_ref, v_ref, o_ref, m_ref, l_ref, acc_ref = refs
    pair = pl.program_id(1)
    qi = pl.program_id(2)
    n_rep = tk // LANES
    causal = _iota2((tq, tk), 0) >= _iota2((tq, tk), 1)
    out = jnp.zeros((tq, HEAD_SLOT), F32)
    for hh in range(2):
        lanes = slice(hh * HEAD_SLOT, (hh + 1) * HEAD_SLOT)
        q = q_ref[:, lanes]
        if has_bias:
            head = 2 * pair + hh
            lane_id = _iota2((tq, LANES), 1)
            bq = jnp.sum(jnp.where(lane_id == head, bq_ref[...], 0.0), axis=1, keepdims=True)
        m_ref[...] = jnp.full_like(m_ref, NEG_BIG)
        l_ref[...] = jnp.zeros_like(l_ref)
        acc_ref[...] = jnp.zeros_like(acc_ref)

        def step(j, masked):
            k0 = pl.multiple_of(j * tk, tk)
            kb = k_ref[pl.ds(k0, tk), lanes]
            vb = v_ref[pl.ds(k0, tk), lanes]
            s_ = _dot_nt(q, kb)
            if has_bias:
                row_id = _iota2((8, tk), 0)
                bk = jnp.sum(jnp.where(row_id == head, bk_ref[j], 0.0), axis=0, keepdims=True)
                s_ = s_ + (bq - bk)
            if masked:
                s_ = jnp.where(causal, s_, NEG_BIG)
            m_prev = m_ref[...]
            m_new = jnp.maximum(m_prev, jnp.max(s_, axis=1, keepdims=True))
            alpha = jnp.exp(m_prev - m_new)
            p = jnp.exp(s_ - jnp.concatenate([m_new] * n_rep, axis=1))
            l_ref[...] = alpha * l_ref[...] + jnp.sum(p, axis=1, keepdims=True)
            acc_ref[...] = alpha * acc_ref[...] + _dot(p.astype(BF), vb)
            m_ref[...] = m_new

        def body(j, carry):
            step(j, False)
            return carry

        lax.fori_loop(0, qi * (tq // tk), body, 0)
        step(qi * (tq // tk), True)
        out = out + acc_ref[...] / l_ref[...]
    o_ref[...] = out.astype(BF)


def _attention(q, k, v, bias_q=None):
    b, s, _ = q.shape
    tq = min(ATT_Q_ROWS, s)
    tk = tq
    has_bias = bias_q is not None
    pairw = 2 * HEAD_SLOT
    in_specs = [
        pl.BlockSpec((None, tq, pairw), lambda bi, p, i: (bi, i, p)),
        pl.BlockSpec((None, s, pairw), lambda bi, p, i: (bi, 0, p)),
        pl.BlockSpec((None, s, pairw), lambda bi, p, i: (bi, 0, p)),
    ]
    args = [q, k, v]
    if has_bias:
        in_specs += [pl.BlockSpec((None, tq, LANES), lambda bi, p, i: (bi, i, 0)),
                     pl.BlockSpec((None, s // tk, 8, tk), lambda bi, p, i: (bi, 0, 0, 0))]
        bias_k = jnp.transpose(bias_q[:, :, :8].reshape(b, s // tk, tk, 8), (0, 1, 3, 2))
        args += [bias_q, bias_k]
    return pl.pallas_call(
        functools.partial(_attn_kernel, tq=tq, tk=tk, has_bias=has_bias),
        grid=(b, N_HEADS // 2, s // tq),
        in_specs=in_specs,
        out_specs=pl.BlockSpec((None, tq, HEAD_SLOT), lambda bi, p, i: (bi, i, p)),
        out_shape=jax.ShapeDtypeStruct((b, s, (N_HEADS // 2) * HEAD_SLOT), BF),
        scratch_shapes=[pltpu.VMEM((tq, LANES), F32), pltpu.VMEM((tq, LANES), F32),
                        pltpu.VMEM((tq, HEAD_SLOT), F32)],
        compiler_params=pltpu.CompilerParams(
            dimension_semantics=("arbitrary", "arbitrary", "arbitrary"),
            vmem_limit_bytes=VMEM_LIMIT_BYTES),
        name="fox_attention" if has_bias else "mla_attention",
    )(*args)


def _mix_out_kernel(x_ref, mod_ref, oa_ref, ob_ref, oc_ref, od_ref, w_ref, lng_ref, lnb_ref, o_ref,
                    *, alpha):
    x = x_ref[...]
    gt = mod_ref[5:6, :]
    y = (_dot(oa_ref[...], w_ref[0]) + _dot(ob_ref[...], w_ref[1])
         + _dot(oc_ref[...], w_ref[2]) + _dot(od_ref[...], w_ref[3]))
    r = alpha * x + (1.0 + gt) * y
    o_ref[...] = _layer_norm(r, lng_ref[...], lnb_ref[...])


def _mix_out(x, mod, oa, ob, oc, od, w_out, ln_g, ln_b, *, alpha):
    b, s, d = x.shape
    gw = GROUP_WIDTH
    tm = min(OUT_ROWS, s)
    w4 = w_out.astype(BF).reshape(4, gw, d)
    tok = lambda n: pl.BlockSpec((None, tm, n), lambda bi, i: (bi, i, 0))
    return pl.pallas_call(
        functools.partial(_mix_out_kernel, alpha=alpha),
        grid=(b, s // tm),
        in_specs=[tok(d), pl.BlockSpec((None, N_MOD, d), lambda bi, i: (bi, 0, 0)),
                  tok(gw), tok(gw), tok(gw), tok(gw),
                  pl.BlockSpec((4, gw, d), lambda bi, i: (0, 0, 0)),
                  pl.BlockSpec((1, d), lambda bi, i: (0, 0)),
                  pl.BlockSpec((1, d), lambda bi, i: (0, 0))],
        out_specs=tok(d),
        out_shape=jax.ShapeDtypeStruct((b, s, d), F32),
        compiler_params=pltpu.CompilerParams(
            dimension_semantics=("arbitrary", "arbitrary"), vmem_limit_bytes=VMEM_LIMIT_BYTES),
        name="mixer_out",
    )(x, mod, oa, ob, oc, od, w4, ln_g.reshape(1, d), ln_b.reshape(1, d))


def kernel(x, c, ada_w, ada_b, ln_g, ln_b, ffn1_w_in, ffn1_w_out, ffn2_w_in, ffn2_w_out, mix_w_in, mix_w_out, hgrn_lb_logits, hgrn_norm_g, mla_q_norm_g, mla_kv_norm_g, mla_w_uq, mla_w_ukv, fox_b_f, gmlp_ln_g, gmlp_ln_b, gmlp_w_s, gmlp_b_s):
    depth = ada_w.shape[0]
    b, s, d = x.shape
    assert d == N_HEADS * GROUP_WIDTH and s % GMLP_CHUNK == 0
    alpha = float((2 * depth) ** 0.25)
    mod_all = _modulation(c, ada_w, ada_b).reshape(depth, b, N_MOD, d)
    ct, st = _rope_tables(s)
    for l in range(depth):
        mod = mod_all[l]
        x = _ffn_sublayer(x, mod, ffn1_w_in[l], ffn1_w_out[l], ln_g[l, 0], ln_b[l, 0],
                          mod_row=0, alpha=alpha)
        w_big, w_uq2, w_ukv2 = _mix_in_weights(mix_w_in[l], mla_w_uq[l], mla_w_ukv[l])
        hg, mq, mk, mv, fq, fk, fv, ff, od = _mix_in(
            x, mod, w_big, w_uq2, w_ukv2, mla_q_norm_g[l], mla_kv_norm_g[l], ct, st, fox_b_f[l],
            gmlp_ln_g[l], gmlp_ln_b[l], gmlp_w_s[l], gmlp_b_s[l])
        oa = _hgrn(hg, hgrn_lb_logits, hgrn_norm_g[l], layer=l)
        ob = _attention(mq, mk, mv)
        oc = _attention(fq, fk, fv, ff)
        x = _mix_out(x, mod, oa, ob, oc, od, mix_w_out[l], ln_g[l, 1], ln_b[l, 1], alpha=alpha)
        x = _ffn_sublayer(x, mod, ffn2_w_in[l], ffn2_w_out[l], ln_g[l, 2], ln_b[l, 2],
                          mod_row=6, alpha=alpha)
    return x
```

```python
import functools

import jax
import jax.numpy as jnp
from jax import lax
from jax.experimental import pallas as pl
from jax.experimental.pallas import tpu as pltpu

BF = jnp.bfloat16
F32 = jnp.float32

N_HEADS = 4
GROUP_WIDTH = 256
HEAD_DIM = 64
HEAD_SHIFT = 6
HGRN_CHUNK = 16
CHUNK_SHIFT = 4
MLA_Q_LORA = 256
MLA_KV_LORA = 128
MLA_NOPE = 64
MLA_ROPE = 32
ROPE_THETA = 10000.0
GMLP_CHUNK = 128
N_MOD = 9
LN_EPS = 1e-5
RMS_EPS = 1e-6
LB_FLOOR = 1e-30
NEG_BIG = -1e30

LANES = 128
HEAD_SLOT = 128
VMEM_LIMIT_BYTES = 56 * 1024 * 1024

FFN_ROWS = 512
FFN_CHUNK = 256
MIX_ROWS = 256
HGRN_ROWS = 128
ATT_Q_ROWS = 256
OUT_ROWS = 512


def _dot(a, b):
    return jnp.dot(a, b, preferred_element_type=F32)


def _dot_nt(a, b):
    return lax.dot_general(a, b, (((1,), (1,)), ((), ())), preferred_element_type=F32)


def _sigmoid(x):
    return jax.nn.sigmoid(x)


def _silu(x):
    return x * _sigmoid(x)


def _log_sigmoid(x):
    return jnp.minimum(x, 0.0) - jnp.log1p(jnp.exp(-jnp.abs(x)))


def _logaddexp(a, b):
    return jnp.maximum(a, b) + jnp.log1p(jnp.exp(-jnp.abs(a - b)))


def _gelu_tanh(x):
    c = 0.7978845608028654
    return 0.5 * x * (1.0 + jnp.tanh(c * (x + 0.044715 * (x * x * x))))


def _layer_norm(r, g, b):
    mu = jnp.mean(r, axis=-1, keepdims=True)
    d = r - mu
    var = jnp.mean(d * d, axis=-1, keepdims=True)
    return d * lax.rsqrt(var + LN_EPS) * g + b


def _rms_norm(r, g):
    return r * lax.rsqrt(jnp.mean(r * r, axis=-1, keepdims=True) + RMS_EPS) * g


def _split3(x):
    hi = x.astype(BF)
    r1 = x - hi.astype(F32)
    mid = r1.astype(BF)
    lo = (r1 - mid.astype(F32)).astype(BF)
    return hi, mid, lo


def _dot_exact_lhs(m_bf, x):
    hi, mid, lo = _split3(x)
    return _dot(m_bf, hi) + _dot(m_bf, mid) + _dot(m_bf, lo)


def _iota2(shape, dim):
    return lax.broadcasted_iota(jnp.int32, shape, dim)


def _mod_kernel(c_ref, w_ref, b_ref, o_ref):
    c = c_ref[...]
    o_ref[...] = _dot(_silu(c).astype(BF), w_ref[...].astype(BF)) + b_ref[...]


def _modulation(c, ada_w, ada_b):
    depth, d, n = ada_w.shape
    b = c.shape[0]
    tn = 1152 if n % 1152 == 0 else n
    return pl.pallas_call(
        _mod_kernel,
        grid=(depth, n // tn),
        in_specs=[
            pl.BlockSpec((b, d), lambda l, j: (0, 0)),
            pl.BlockSpec((None, d, tn), lambda l, j: (l, 0, j)),
            pl.BlockSpec((None, 1, tn), lambda l, j: (l, 0, j)),
        ],
        out_specs=pl.BlockSpec((None, b, tn), lambda l, j: (l, 0, j)),
        out_shape=jax.ShapeDtypeStruct((depth, b, n), F32),
        compiler_params=pltpu.CompilerParams(
            dimension_semantics=("arbitrary", "arbitrary"), vmem_limit_bytes=VMEM_LIMIT_BYTES),
        name="adaln_modulation",
    )(c, ada_w, ada_b.reshape(depth, 1, n))


def _ffn_kernel(x_ref, mod_ref, wi_ref, wo_ref, lng_ref, lnb_ref, o_ref, acc_ref,
                *, mod_row, dff, ck, alpha):
    x = x_ref[...]
    sh = mod_ref[mod_row:mod_row + 1, :]
    sc = mod_ref[mod_row + 1:mod_row + 2, :]
    gt = mod_ref[mod_row + 2:mod_row + 3, :]
    h = (x * (1.0 + sc) + sh).astype(BF)
    for c in range(dff // ck):
        gate = _dot(h, wi_ref[:, c * ck:(c + 1) * ck])
        up = _dot(h, wi_ref[:, dff + c * ck:dff + (c + 1) * ck])
        a = (_silu(gate) * up).astype(BF)
        y = _dot(a, wo_ref[c * ck:(c + 1) * ck, :])
        if c == 0:
            acc_ref[...] = y
        else:
            acc_ref[...] += y
    r = alpha * x + (0.5 * (1.0 + gt)) * acc_ref[...]
    o_ref[...] = _layer_norm(r, lng_ref[...], lnb_ref[...])


def _ffn_sublayer(x, mod, w_in, w_out, ln_g, ln_b, *, mod_row, alpha):
    b, s, d = x.shape
    dff = w_out.shape[0]
    ck = FFN_CHUNK if dff % FFN_CHUNK == 0 else dff
    tm = min(FFN_ROWS, s)
    const2 = lambda bi, i: (0, 0)
    return pl.pallas_call(
        functools.partial(_ffn_kernel, mod_row=mod_row, dff=dff, ck=ck, alpha=alpha),
        grid=(b, s // tm),
        in_specs=[
            pl.BlockSpec((None, tm, d), lambda bi, i: (bi, i, 0)),
            pl.BlockSpec((None, N_MOD, d), lambda bi, i: (bi, 0, 0)),
            pl.BlockSpec((d, 2 * dff), const2),
            pl.BlockSpec((dff, d), const2),
            pl.BlockSpec((1, d), lambda bi, i: (0, 0)),
            pl.BlockSpec((1, d), lambda bi, i: (0, 0)),
        ],
        out_specs=pl.BlockSpec((None, tm, d), lambda bi, i: (bi, i, 0)),
        out_shape=jax.ShapeDtypeStruct((b, s, d), F32),
        scratch_shapes=[pltpu.VMEM((tm, d), F32)],
        compiler_params=pltpu.CompilerParams(
            dimension_semantics=("arbitrary", "arbitrary"), vmem_limit_bytes=VMEM_LIMIT_BYTES),
        name="ffn_sublayer",
    )(x, mod, w_in.astype(BF), w_out.astype(BF), ln_g.reshape(1, d), ln_b.reshape(1, d))


_C_HG = 0
_C_CQ = 1024
_C_CKV = 1280
_C_KR = 1408
_C_KRS = 1536
_C_FQ = 1664
_C_FK = 2176
_C_FV = 2688
_C_FF = 3200
_C_DU = 3328
_C_DV = 3584
_C_END = 3840


def _mix_in_kernel(x_ref, mod_ref, w_ref, wuq_ref, wukv_ref, qg_ref, kvg_ref, ct_ref, st_ref,
                   fb_ref, dlg_ref, dlb_ref, ws_ref, bs_ref,
                   hg_ref, mq_ref, mk_ref, mv_ref, fq_ref, fk_ref, fv_ref, ff_ref, od_ref,
                   carry_ref, *, tm, mla_scale, fox_scale):
    x = x_ref[...]
    sh = mod_ref[3:4, :]
    sc = mod_ref[4:5, :]
    h = (x * (1.0 + sc) + sh).astype(BF)

    def proj(lo, hi):
        return _dot(h, w_ref[:, lo:hi])

    hg_ref[...] = proj(_C_HG, _C_CQ)

    ct = ct_ref[...]
    st = st_ref[...]
    ct4 = jnp.concatenate([ct] * N_HEADS, axis=1)
    st4 = jnp.concatenate([st] * N_HEADS, axis=1)
    nq = _rms_norm(proj(_C_CQ, _C_CKV), qg_ref[...]).astype(BF)
    q2 = _dot(nq, wuq_ref[...])
    nslot = N_HEADS * HEAD_SLOT
    q_rot = q2[:, :nslot] * ct4 + q2[:, nslot:] * st4
    mq_ref[...] = (q_rot * mla_scale).astype(BF)
    nkv = _rms_norm(proj(_C_CKV, _C_KR), kvg_ref[...]).astype(BF)
    kv2 = _dot(nkv, wukv_ref[...])
    kr_rot = proj(_C_KR, _C_KRS) * ct + proj(_C_KRS, _C_FQ) * st
    mk_ref[...] = (kv2[:, :nslot] + jnp.concatenate([kr_rot] * N_HEADS, axis=1)).astype(BF)
    mv_ref[...] = kv2[:, nslot:].astype(BF)

    fq_ref[...] = (proj(_C_FQ, _C_FK) * fox_scale).astype(BF)
    fk_ref[...] = proj(_C_FK, _C_FV).astype(BF)
    fv_ref[...] = proj(_C_FV, _C_FF).astype(BF)
    logf = _log_sigmoid(proj(_C_FF, _C_DU) + fb_ref[...])
    tri = (_iota2((tm, tm), 0) >= _iota2((tm, tm), 1)).astype(BF)

    @pl.when(pl.program_id(1) == 0)
    def _():
        carry_ref[...] = jnp.zeros_like(carry_ref)

    cum = _dot_exact_lhs(tri, logf) + carry_ref[...]
    ff_ref[...] = cum
    carry_ref[...] = cum[tm - 1:tm, :]

    u = _gelu_tanh(proj(_C_DU, _C_DV))
    vn = _layer_norm(_gelu_tanh(proj(_C_DV, _C_END)), dlg_ref[...], dlb_ref[...])
    cc = GMLP_CHUNK
    causal = _iota2((cc, cc), 0) >= _iota2((cc, cc), 1)
    lane_group = _iota2((cc, GROUP_WIDTH), 1) >> HEAD_SHIFT
    for j in range(tm // cc):
        vj = vn[j * cc:(j + 1) * cc, :]
        mixed = bs_ref[...]
        for g in range(N_HEADS):
            wg = jnp.where(causal, ws_ref[g], 0.0).astype(BF)
            mixed = mixed + _dot(wg, jnp.where(lane_group == g, vj, 0.0).astype(BF))
        od_ref[j * cc:(j + 1) * cc, :] = (u[j * cc:(j + 1) * cc, :] * mixed).astype(BF)


def _place_heads(w, width, offsets):
    k = w.shape[0]
    cols = []
    for hd in range(N_HEADS):
        blk = w[:, hd * width:(hd + 1) * width]
        off = offsets[hd]
        cols.append(jnp.pad(blk, ((0, 0), (off, HEAD_SLOT - off - width))))
    return jnp.concatenate(cols, axis=1).reshape(k, N_HEADS * HEAD_SLOT)


def _mix_in_weights(mix_w_in, mla_w_uq, mla_w_ukv):
    d = mix_w_in.shape[0]
    gw = GROUP_WIDTH
    sizes = (gw, gw, gw, gw, MLA_Q_LORA, MLA_KV_LORA, MLA_ROPE, gw, gw, gw, N_HEADS, gw, gw)
    parts, o = [], 0
    for sz in sizes:
        parts.append(mix_w_in[:, o:o + sz])
        o += sz
    (a_q, a_f, a_i, a_g, b_cq, b_ckv, b_kr, c_q, c_k, c_v, c_f, d_u, d_v) = parts
    half = MLA_ROPE // 2
    z = lambda n: jnp.zeros((d, n), mix_w_in.dtype)
    kr = jnp.concatenate([z(MLA_NOPE), b_kr, z(HEAD_SLOT - MLA_NOPE - MLA_ROPE)], axis=1)
    krs = jnp.concatenate([z(MLA_NOPE), b_kr[:, half:], b_kr[:, :half],
                           z(HEAD_SLOT - MLA_NOPE - MLA_ROPE)], axis=1)
    zero_off = (0,) * N_HEADS
    pair_off = tuple((hd % 2) * HEAD_DIM for hd in range(N_HEADS))
    w_big = jnp.concatenate([
        a_q, a_f, a_i, a_g, b_cq, b_ckv, kr, krs,
        _place_heads(c_q, HEAD_DIM, zero_off), _place_heads(c_k, HEAD_DIM, zero_off),
        _place_heads(c_v, HEAD_DIM, pair_off),
        jnp.pad(c_f, ((0, 0), (0, LANES - N_HEADS))), d_u, d_v], axis=1).astype(BF)
    assert w_big.shape[1] == _C_END
    qk = MLA_NOPE + MLA_ROPE
    uq = mla_w_uq.reshape(MLA_Q_LORA, N_HEADS, qk)
    zq = lambda n: jnp.zeros((MLA_Q_LORA, N_HEADS, n), mla_w_uq.dtype)
    uq_n = jnp.concatenate([uq, zq(HEAD_SLOT - qk)], axis=2)
    uq_s = jnp.concatenate([zq(MLA_NOPE), uq[:, :, MLA_NOPE + half:], uq[:, :, MLA_NOPE:MLA_NOPE + half],
                            zq(HEAD_SLOT - qk)], axis=2)
    w_uq2 = jnp.concatenate([uq_n.reshape(MLA_Q_LORA, -1), uq_s.reshape(MLA_Q_LORA, -1)], axis=1).astype(BF)
    ukv = mla_w_ukv.reshape(MLA_KV_LORA, N_HEADS, MLA_NOPE + HEAD_DIM)
    k_n = _place_heads(ukv[:, :, :MLA_NOPE].reshape(MLA_KV_LORA, -1), MLA_NOPE, zero_off)
    v_p = _place_heads(ukv[:, :, MLA_NOPE:].reshape(MLA_KV_LORA, -1), HEAD_DIM, pair_off)
    w_ukv2 = jnp.concatenate([k_n, v_p], axis=1).astype(BF)
    return w_big, w_uq2, w_ukv2


def _rope_tables(s):
    half = MLA_ROPE // 2
    inv_freq = ROPE_THETA ** (-jnp.arange(half, dtype=F32) / half)
    ang = jnp.arange(s, dtype=F32)[:, None] * inv_freq[None, :]
    cos, sin = jnp.cos(ang), jnp.sin(ang)
    tail = HEAD_SLOT - MLA_NOPE - MLA_ROPE
    ct = jnp.concatenate([jnp.ones((s, MLA_NOPE), F32), cos, cos, jnp.zeros((s, tail), F32)], axis=1)
    st = jnp.concatenate([jnp.zeros((s, MLA_NOPE), F32), -sin, sin, jnp.zeros((s, tail), F32)], axis=1)
    return ct, st


def _mix_in(x, mod, w_big, w_uq2, w_ukv2, q_norm_g, kv_norm_g, ct, st, fox_b_f,
            gmlp_ln_g, gmlp_ln_b, gmlp_w_s, gmlp_b_s):
    b, s, d = x.shape
    tm = min(MIX_ROWS, s)
    nslot = N_HEADS * HEAD_SLOT
    gw = GROUP_WIDTH
    fb = jnp.pad(fox_b_f.reshape(1, N_HEADS), ((0, 0), (0, LANES - N_HEADS)))
    bias_tile = jnp.repeat(gmlp_b_s.T, HEAD_DIM, axis=1)
    full2 = lambda bi, i: (0, 0)
    tok = lambda n: pl.BlockSpec((None, tm, n), lambda bi, i: (bi, i, 0))
    shp = lambda n, dt: jax.ShapeDtypeStruct((b, s, n), dt)
    return pl.pallas_call(
        functools.partial(_mix_in_kernel, tm=tm, mla_scale=float((MLA_NOPE + MLA_ROPE) ** -0.5),
                          fox_scale=float(HEAD_DIM ** -0.5)),
        grid=(b, s // tm),
        in_specs=[
            tok(d),
            pl.BlockSpec((None, N_MOD, d), lambda bi, i: (bi, 0, 0)),
            pl.BlockSpec(w_big.shape, full2),
            pl.BlockSpec(w_uq2.shape, full2),
            pl.BlockSpec(w_ukv2.shape, full2),
            pl.BlockSpec((1, MLA_Q_LORA), full2),
            pl.BlockSpec((1, MLA_KV_LORA), full2),
            pl.BlockSpec((tm, HEAD_SLOT), lambda bi, i: (i, 0)),
            pl.BlockSpec((tm, HEAD_SLOT), lambda bi, i: (i, 0)),
            pl.BlockSpec((1, LANES), full2),
            pl.BlockSpec((1, gw), full2),
            pl.BlockSpec((1, gw), full2),
            pl.BlockSpec(gmlp_w_s.shape, lambda bi, i: (0, 0, 0)),
            pl.BlockSpec(bias_tile.shape, full2),
        ],
        out_specs=[tok(4 * gw), tok(nslot), tok(nslot), tok(nslot), tok(nslot), tok(nslot), tok(nslot),
                   tok(LANES), tok(gw)],
        out_shape=[shp(4 * gw, F32), shp(nslot, BF), shp(nslot, BF), shp(nslot, BF),
                   shp(nslot, BF), shp(nslot, BF), shp(nslot, BF), shp(LANES, F32), shp(gw, BF)],
        scratch_shapes=[pltpu.VMEM((1, LANES), F32)],
        compiler_params=pltpu.CompilerParams(
            dimension_semantics=("arbitrary", "arbitrary"), vmem_limit_bytes=VMEM_LIMIT_BYTES),
        name="mixer_in",
    )(x, mod, w_big, w_uq2, w_ukv2, q_norm_g.reshape(1, -1), kv_norm_g.reshape(1, -1), ct, st, fb,
      gmlp_ln_g.reshape(1, gw), gmlp_ln_b.reshape(1, gw), gmlp_w_s, bias_tile)


def _hgrn_kernel(hg_ref, lbl_ref, ng_ref, o_ref, st_ref, p_ref, oacc_ref, *, layer, depth, tb):
    gw = GROUP_WIDTH
    ch = HGRN_CHUNK
    n_ch = tb // ch

    @pl.when(pl.program_id(1) == 0)
    def _():
        st_ref[...] = jnp.zeros_like(st_ref)

    rows = [lbl_ref[j:j + 1, :] for j in range(depth)]
    top = functools.reduce(jnp.maximum, rows)
    ex = [jnp.exp(r - top) for r in rows]
    sm = [e / sum(ex) for e in ex]
    lb = sum(sm[:layer + 1]) - sm[0]
    lb_floor = jnp.maximum(lb, LB_FLOOR)

    z_q = hg_ref[:, 0:gw]
    z_f = hg_ref[:, gw:2 * gw]
    val = hg_ref[:, 2 * gw:3 * gw]
    z_g = hg_ref[:, 3 * gw:4 * gw]
    log_f = _logaddexp(jnp.log(lb_floor), jnp.log1p(-lb) + _log_sigmoid(z_f))
    kk = (1.0 - lb) * _sigmoid(-z_f) - (lb_floor - lb)
    qf = _silu(z_q)

    r_i = _iota2((tb, tb), 0)
    c_i = _iota2((tb, tb), 1)
    same_chunk = (r_i >> CHUNK_SHIFT) == (c_i >> CHUNK_SHIFT)
    tri_blk = (same_chunk & (r_i >= c_i)).astype(BF)
    ones_blk = same_chunk.astype(BF)
    hi, mid, lo = _split3(log_f)
    g_cum = _dot(tri_blk, hi) + _dot(tri_blk, mid) + _dot(tri_blk, lo)
    g_last = _dot(ones_blk, hi) + _dot(ones_blk, mid) + _dot(ones_blk, lo)
    q_dec = (qf * jnp.exp(g_cum)).astype(BF)
    k_end = kk * jnp.exp(g_last - g_cum)
    dec = jnp.exp(g_last)

    head_r = _iota2((gw, gw), 0) >> HEAD_SHIFT
    head_c = _iota2((gw, gw), 1) >> HEAD_SHIFT
    same_head = head_r == head_c
    head_ones = same_head.astype(BF)

    val_t = val.T.astype(BF)
    row_chunk = _iota2((tb, gw), 0) >> CHUNK_SHIFT
    t_loc = _iota2((ch, gw), 0)

    for c in range(n_ch):
        r0 = c * ch
        g_c = g_cum[r0:r0 + ch, :]
        q_c = qf[r0:r0 + ch, :]
        k_c = kk[r0:r0 + ch, :]
        v_c = val[r0:r0 + ch, :]
        for s_ in range(ch):
            e_ts = jnp.where(t_loc >= s_, jnp.exp(g_c - g_c[s_:s_ + 1, :]), 0.0)
            p_ref[s_ * ch:(s_ + 1) * ch, :] = (e_ts * q_c * k_c[s_:s_ + 1, :]).astype(BF)
        scores = _dot(p_ref[...], head_ones)
        o_c = _dot_nt(q_dec[r0:r0 + ch, :], st_ref[...].astype(BF))
        for s_ in range(ch):
            o_c = o_c + scores[s_ * ch:(s_ + 1) * ch, :] * v_c[s_:s_ + 1, :]
        oacc_ref[r0:r0 + ch, :] = o_c
        k_m = jnp.where(row_chunk == c, k_end, 0.0).astype(BF)
        kv_t = jnp.where(same_head, _dot(val_t, k_m), 0.0)
        st_ref[...] = st_ref[...] * dec[r0:r0 + 1, :] + kv_t

    o = oacc_ref[...]
    h2, l2, _ = _split3(o * o)
    ms = (_dot(h2, head_ones) + _dot(l2, head_ones)) * (1.0 / HEAD_DIM)
    o = o * lax.rsqrt(ms + RMS_EPS) * ng_ref[...]
    o_ref[...] = (o * _silu(z_g)).astype(BF)


def _hgrn(hg, lb_logits, norm_g, *, layer):
    b, s, _ = hg.shape
    depth = lb_logits.shape[0]
    gw = GROUP_WIDTH
    tb = min(HGRN_ROWS, s)
    return pl.pallas_call(
        functools.partial(_hgrn_kernel, layer=layer, depth=depth, tb=tb),
        grid=(b, s // tb),
        in_specs=[
            pl.BlockSpec((None, tb, 4 * gw), lambda bi, i: (bi, i, 0)),
            pl.BlockSpec((depth, gw), lambda bi, i: (0, 0)),
            pl.BlockSpec((1, gw), lambda bi, i: (0, 0)),
        ],
        out_specs=pl.BlockSpec((None, tb, gw), lambda bi, i: (bi, i, 0)),
        out_shape=jax.ShapeDtypeStruct((b, s, gw), BF),
        scratch_shapes=[pltpu.VMEM((gw, gw), F32),
                        pltpu.VMEM((HGRN_CHUNK * HGRN_CHUNK, gw), BF),
                        pltpu.VMEM((tb, gw), F32)],
        compiler_params=pltpu.CompilerParams(
            dimension_semantics=("arbitrary", "arbitrary"), vmem_limit_bytes=VMEM_LIMIT_BYTES),
        name="hgrn2",
    )(hg, lb_logits, norm_g.reshape(1, gw))


def _attn_kernel(*refs, s_len, tq, has_bias):
    if has_bias:
        q_ref, k_ref, v_ref, bq_ref, bk_ref, o_ref, s_ref = refs
    else:
        q_ref, k_ref, v_ref, o_ref, s_ref = refs
    pair = pl.program_id(1)
    n_q = s_len // tq
    n_g = tq // LANES
    causal = _iota2((tq, tq), 0) >= _iota2((tq, tq), 1)
    for qi in range(n_q):
        rows = slice(qi * tq, (qi + 1) * tq)
        out = None
        for hh in range(2):
            lanes = slice(hh * HEAD_SLOT, (hh + 1) * HEAD_SLOT)
            stage = s_ref.at[2 * (qi % 2) + hh]
            q = q_ref[rows, lanes]
            if has_bias:
                head = 2 * pair + hh
                lane_id = _iota2((tq, LANES), 1)
                bq = jnp.sum(jnp.where(lane_id == head, bq_ref[rows, :], 0.0), axis=1, keepdims=True)
            m_part = None
            for j in range(qi + 1):
                cols = slice(j * tq, (j + 1) * tq)
                sj = _dot_nt(q, k_ref[cols, lanes])
                if has_bias:
                    row_id = _iota2((8, tq), 0)
                    bk = jnp.sum(jnp.where(row_id == head, bk_ref[:, cols], 0.0), axis=0, keepdims=True)
                    sj = sj + (bq - bk)
                if j == qi:
                    sj = jnp.where(causal, sj, NEG_BIG)
                stage[:, cols] = sj
                for g in range(n_g):
                    blk = sj[:, g * LANES:(g + 1) * LANES]
                    m_part = blk if m_part is None else jnp.maximum(m_part, blk)
            m_b = jnp.broadcast_to(jnp.max(m_part, axis=1, keepdims=True), (tq, LANES))
            l_part = jnp.zeros((tq, LANES), F32)
            acc = jnp.zeros((tq, HEAD_SLOT), F32)
            for j in range(qi + 1):
                ps = []
                for g in range(n_g):
                    c0 = j * tq + g * LANES
                    p = jnp.exp(stage[:, c0:c0 + LANES] - m_b)
                    l_part = l_part + p
                    ps.append(p.astype(BF))
                acc = acc + _dot(jnp.concatenate(ps, axis=1), v_ref[j * tq:(j + 1) * tq, lanes])
            o_h = acc * (1.0 / jnp.sum(l_part, axis=1, keepdims=True))
            out = o_h if out is None else out + o_h
        o_ref[rows, :] = out.astype(BF)


def _attention(q, k, v, bias_q=None):
    b, s, _ = q.shape
    tq = min(ATT_Q_ROWS, s)
    has_bias = bias_q is not None
    pairw = 2 * HEAD_SLOT
    seq = lambda n: pl.BlockSpec((None, s, n), lambda bi, p: (bi, 0, p))
    in_specs = [seq(pairw), seq(pairw), seq(pairw)]
    args = [q, k, v]
    if has_bias:
        in_specs += [pl.BlockSpec((None, s, LANES), lambda bi, p: (bi, 0, 0)),
                     pl.BlockSpec((None, 8, s), lambda bi, p: (bi, 0, 0))]
        args += [bias_q, jnp.transpose(bias_q[:, :, :8], (0, 2, 1))]
    return pl.pallas_call(
        functools.partial(_attn_kernel, s_len=s, tq=tq, has_bias=has_bias),
        grid=(b, N_HEADS // 2),
        in_specs=in_specs,
        out_specs=seq(HEAD_SLOT),
        out_shape=jax.ShapeDtypeStruct((b, s, (N_HEADS // 2) * HEAD_SLOT), BF),
        scratch_shapes=[pltpu.VMEM((4, tq, s), F32)],
        compiler_params=pltpu.CompilerParams(
            dimension_semantics=("arbitrary", "arbitrary"), vmem_limit_bytes=VMEM_LIMIT_BYTES),
        name="fox_attention" if has_bias else "mla_attention",
    )(*args)


def _mix_out_kernel(x_ref, mod_ref, oa_ref, ob_ref, oc_ref, od_ref, w_ref, lng_ref, lnb_ref, o_ref,
                    *, alpha):
    x = x_ref[...]
    gt = mod_ref[5:6, :]
    y = (_dot(oa_ref[...], w_ref[0]) + _dot(ob_ref[...], w_ref[1])
         + _dot(oc_ref[...], w_ref[2]) + _dot(od_ref[...], w_ref[3]))
    r = alpha * x + (1.0 + gt) * y
    o_ref[...] = _layer_norm(r, lng_ref[...], lnb_ref[...])


def _mix_out(x, mod, oa, ob, oc, od, w_out, ln_g, ln_b, *, alpha):
    b, s, d = x.shape
    gw = GROUP_WIDTH
    tm = min(OUT_ROWS, s)
    w4 = w_out.astype(BF).reshape(4, gw, d)
    tok = lambda n: pl.BlockSpec((None, tm, n), lambda bi, i: (bi, i, 0))
    return pl.pallas_call(
        functools.partial(_mix_out_kernel, alpha=alpha),
        grid=(b, s // tm),
        in_specs=[tok(d), pl.BlockSpec((None, N_MOD, d), lambda bi, i: (bi, 0, 0)),
                  tok(gw), tok(gw), tok(gw), tok(gw),
                  pl.BlockSpec((4, gw, d), lambda bi, i: (0, 0, 0)),
                  pl.BlockSpec((1, d), lambda bi, i: (0, 0)),
                  pl.BlockSpec((1, d), lambda bi, i: (0, 0))],
        out_specs=tok(d),
        out_shape=jax.ShapeDtypeStruct((b, s, d), F32),
        compiler_params=pltpu.CompilerParams(
            dimension_semantics=("arbitrary", "arbitrary"), vmem_limit_bytes=VMEM_LIMIT_BYTES),
        name="mixer_out",
    )(x, mod, oa, ob, oc, od, w4, ln_g.reshape(1, d), ln_b.reshape(1, d))


def kernel(x, c, ada_w, ada_b, ln_g, ln_b, ffn1_w_in, ffn1_w_out, ffn2_w_in, ffn2_w_out, mix_w_in, mix_w_out, hgrn_lb_logits, hgrn_norm_g, mla_q_norm_g, mla_kv_norm_g, mla_w_uq, mla_w_ukv, fox_b_f, gmlp_ln_g, gmlp_ln_b, gmlp_w_s, gmlp_b_s):
    depth = ada_w.shape[0]
    b, s, d = x.shape
    assert d == N_HEADS * GROUP_WIDTH and s % GMLP_CHUNK == 0
    alpha = float((2 * depth) ** 0.25)
    mod_all = _modulation(c, ada_w, ada_b).reshape(depth, b, N_MOD, d)
    ct, st = _rope_tables(s)
    for l in range(depth):
        mod = mod_all[l]
        x = _ffn_sublayer(x, mod, ffn1_w_in[l], ffn1_w_out[l], ln_g[l, 0], ln_b[l, 0],
                          mod_row=0, alpha=alpha)
        w_big, w_uq2, w_ukv2 = _mix_in_weights(mix_w_in[l], mla_w_uq[l], mla_w_ukv[l])
        hg, mq, mk, mv, fq, fk, fv, ff, od = _mix_in(
            x, mod, w_big, w_uq2, w_ukv2, mla_q_norm_g[l], mla_kv_norm_g[l], ct, st, fox_b_f[l],
            gmlp_ln_g[l], gmlp_ln_b[l], gmlp_w_s[l], gmlp_b_s[l])
        oa = _hgrn(hg, hgrn_lb_logits, hgrn_norm_g[l], layer=l)
        ob = _attention(mq, mk, mv)
        oc = _attention(fq, fk, fv, ff)
        x = _mix_out(x, mod, oa, ob, oc, od, mix_w_out[l], ln_g[l, 1], ln_b[l, 1], alpha=alpha)
        x = _ffn_sublayer(x, mod, ffn2_w_in[l], ffn2_w_out[l], ln_g[l, 2], ln_b[l, 2],
                          mod_row=6, alpha=alpha)
    return x
```

```python
import functools

import jax
import jax.numpy as jnp
from jax import lax
from jax.experimental import pallas as pl
from jax.experimental.pallas import tpu as pltpu

BF = jnp.bfloat16
F32 = jnp.float32

N_HEADS = 4
GROUP_WIDTH = 256
HEAD_DIM = 64
HEAD_SHIFT = 6
HGRN_CHUNK = 16
CHUNK_SHIFT = 4
MLA_Q_LORA = 256
MLA_KV_LORA = 128
MLA_NOPE = 64
MLA_ROPE = 32
ROPE_THETA = 10000.0
GMLP_CHUNK = 128
N_MOD = 9
LN_EPS = 1e-5
RMS_EPS = 1e-6
LB_FLOOR = 1e-30
NEG_BIG = -1e30

LANES = 128
SUBLANES = 8
HEAD_SLOT = 128
VMEM_LIMIT_BYTES = 56 * 1024 * 1024

FFN_ROWS = 512
FFN_CHUNK = 256
MIX_ROWS = 512
HGRN_ROWS = 256
ATT_Q_ROWS = 256


def _dot(a, b):
    return jnp.dot(a, b, preferred_element_type=F32)


def _dot_nt(a, b):
    return lax.dot_general(a, b, (((1,), (1,)), ((), ())), preferred_element_type=F32)


def _dot_tn(a, b):
    return lax.dot_general(a, b, (((0,), (0,)), ((), ())), preferred_element_type=F32)


def _sigmoid(x):
    return jax.nn.sigmoid(x)


def _silu(x):
    return x * _sigmoid(x)


def _log_sigmoid(x):
    return jnp.minimum(x, 0.0) - jnp.log1p(jnp.exp(-jnp.abs(x)))


def _gelu_tanh(x):
    c = 0.7978845608028654
    return 0.5 * x * (1.0 + jnp.tanh(c * (x + 0.044715 * (x * x * x))))


def _layer_norm(r, g, b):
    mu = jnp.mean(r, axis=-1, keepdims=True)
    d = r - mu
    var = jnp.mean(d * d, axis=-1, keepdims=True)
    return d * lax.rsqrt(var + LN_EPS) * g + b


def _rms_norm(r, g):
    return r * lax.rsqrt(jnp.mean(r * r, axis=-1, keepdims=True) + RMS_EPS) * g


def _split3(x):
    hi = x.astype(BF)
    r1 = x - hi.astype(F32)
    mid = r1.astype(BF)
    lo = (r1 - mid.astype(F32)).astype(BF)
    return hi, mid, lo


def _iota2(shape, dim):
    return lax.broadcasted_iota(jnp.int32, shape, dim)


def _resident(shape, index_map):
    return pl.BlockSpec(shape, index_map, pipeline_mode=pl.Buffered(1))


def _params(n_axes):
    return pltpu.CompilerParams(dimension_semantics=("arbitrary",) * n_axes,
                                vmem_limit_bytes=VMEM_LIMIT_BYTES)


def _mod_kernel(c_ref, w_ref, b_ref, o_ref):
    c = c_ref[...]
    o_ref[...] = _dot(_silu(c).astype(BF), w_ref[...].astype(BF)) + b_ref[...]


def _modulation(c, ada_w, ada_b):
    depth, d, n = ada_w.shape
    b = c.shape[0]
    tn = 1152 if n % 1152 == 0 else n
    return pl.pallas_call(
        _mod_kernel,
        grid=(depth, n // tn),
        in_specs=[
            pl.BlockSpec((b, d), lambda l, j: (0, 0)),
            pl.BlockSpec((None, d, tn), lambda l, j: (l, 0, j)),
            pl.BlockSpec((None, 1, tn), lambda l, j: (l, 0, j)),
        ],
        out_specs=pl.BlockSpec((None, b, tn), lambda l, j: (l, 0, j)),
        out_shape=jax.ShapeDtypeStruct((depth, b, n), F32),
        compiler_params=_params(2),
        name="adaln_modulation",
    )(c, ada_w, ada_b.reshape(depth, 1, n))


def _ffn_kernel(*refs, mod_row, ln_row, dff, ck, alpha, with_mix):
    if with_mix:
        (x_ref, mod_ref, oa_ref, ob_ref, oc_ref, od_ref, wm_ref, wi_ref, wo_ref, lng_ref, lnb_ref,
         o_ref, acc_ref) = refs
    else:
        x_ref, mod_ref, wi_ref, wo_ref, lng_ref, lnb_ref, o_ref, acc_ref = refs
    x = x_ref[...]
    if with_mix:
        gw = GROUP_WIDTH
        y = (_dot(oa_ref[...], wm_ref[0:gw, :]) + _dot(ob_ref[...], wm_ref[gw:2 * gw, :])
             + _dot(oc_ref[...], wm_ref[2 * gw:3 * gw, :]) + _dot(od_ref[...], wm_ref[3 * gw:4 * gw, :]))
        x = _layer_norm(alpha * x + (1.0 + mod_ref[5:6, :]) * y,
                        lng_ref[ln_row - 1:ln_row, :], lnb_ref[ln_row - 1:ln_row, :])
    sh = mod_ref[mod_row:mod_row + 1, :]
    sc = mod_ref[mod_row + 1:mod_row + 2, :]
    gt = mod_ref[mod_row + 2:mod_row + 3, :]
    h = (x * (1.0 + sc) + sh).astype(BF)
    for c in range(dff // ck):
        gate = _dot(h, wi_ref[:, c * ck:(c + 1) * ck])
        up = _dot(h, wi_ref[:, dff + c * ck:dff + (c + 1) * ck])
        a = (_silu(gate) * up).astype(BF)
        y = _dot(a, wo_ref[c * ck:(c + 1) * ck, :])
        if c == 0:
            acc_ref[...] = y
        else:
            acc_ref[...] += y
    r = alpha * x + (0.5 * (1.0 + gt)) * acc_ref[...]
    o_ref[...] = _layer_norm(r, lng_ref[ln_row:ln_row + 1, :], lnb_ref[ln_row:ln_row + 1, :])


def _ffn_sublayer(x, mod, w_in, w_out, ln_g, ln_b, layer, *, mod_row, ln_row, alpha, mix=None):
    b, s, d = x.shape
    dff = w_out.shape[1]
    ck = FFN_CHUNK if dff % FFN_CHUNK == 0 else dff
    tm = min(FFN_ROWS, s)
    tok = lambda n: pl.BlockSpec((None, tm, n), lambda bi, i: (bi, i, 0))
    lay = lambda bi, i: (layer, 0, 0)
    in_specs = [tok(d), pl.BlockSpec((None, N_MOD, d), lambda bi, i: (bi, 0, 0))]
    args = [x, mod]
    if mix is not None:
        oa, ob, oc, od, w_mix = mix
        gw = GROUP_WIDTH
        in_specs += [tok(gw), tok(gw), tok(gw), tok(gw), _resident((None, d, d), lay)]
        args += [oa, ob, oc, od, w_mix]
    in_specs += [_resident((None, d, 2 * dff), lay), _resident((None, dff, d), lay),
                 _resident((None,) + ln_g.shape[1:], lay), _resident((None,) + ln_b.shape[1:], lay)]
    args += [w_in, w_out, ln_g, ln_b]
    return pl.pallas_call(
        functools.partial(_ffn_kernel, mod_row=mod_row, ln_row=ln_row, dff=dff, ck=ck, alpha=alpha,
                          with_mix=mix is not None),
        grid=(b, s // tm),
        in_specs=in_specs,
        out_specs=tok(d),
        out_shape=jax.ShapeDtypeStruct((b, s, d), F32),
        scratch_shapes=[pltpu.VMEM((tm, d), F32)],
        compiler_params=_params(2),
        name="mix_out_ffn" if mix is not None else "ffn_sublayer",
    )(*args)


_C_HG = 0
_C_CQ = 1024
_C_CKV = 1280
_C_KR = 1408
_C_FQ = 1536
_C_FK = 1792
_C_FV = 2048
_C_FF = 2304
_C_DU = 2432
_C_DV = 2688
_C_END = 2944


def _mix_in_kernel(x_ref, mod_ref, w_ref, wuq_ref, wukv_ref, qg_ref, kvg_ref, ct_ref, st_ref,
                   fb_ref, dlg_ref, dlb_ref, ws_ref, bs_ref,
                   hg_ref, mq_ref, mk_ref, mv_ref, fq_ref, fk_ref, fv_ref, ff_ref, fft_ref, od_ref,
                   carry_ref, *, tm, mla_scale, fox_scale):
    x = x_ref[...]
    sh = mod_ref[3:4, :]
    sc = mod_ref[4:5, :]
    h = (x * (1.0 + sc) + sh).astype(BF)

    def proj(lo, hi):
        return _dot(h, w_ref[:, lo:hi])

    hg_ref[...] = proj(_C_HG, _C_CQ)

    ct = ct_ref[...]
    st = st_ref[...]
    ct4 = jnp.concatenate([ct] * N_HEADS, axis=1)
    st4 = jnp.concatenate([st] * N_HEADS, axis=1)
    nq = _rms_norm(proj(_C_CQ, _C_CKV), qg_ref[...]).astype(BF)
    q2 = _dot(nq, wuq_ref[...])
    nslot = N_HEADS * HEAD_SLOT
    q_rot = q2[:, :nslot] * ct4 + q2[:, nslot:] * st4
    mq_ref[...] = (q_rot * mla_scale).astype(BF)
    nkv = _rms_norm(proj(_C_CKV, _C_KR), kvg_ref[...]).astype(BF)
    kv2 = _dot(nkv, wukv_ref[...])
    kr = proj(_C_KR, _C_FQ)
    half = MLA_ROPE // 2
    first_half = _iota2((tm, HEAD_SLOT), 1) < MLA_NOPE + half
    kr_swap = jnp.where(first_half, pltpu.roll(kr, HEAD_SLOT - half, 1), pltpu.roll(kr, half, 1))
    kr_rot = kr * ct + kr_swap * st
    mk_ref[...] = (kv2[:, :nslot] + jnp.concatenate([kr_rot] * N_HEADS, axis=1)).astype(BF)
    mv_ref[...] = kv2[:, nslot:].astype(BF)

    fq_ref[...] = (proj(_C_FQ, _C_FK) * fox_scale).astype(BF)
    fk_ref[...] = proj(_C_FK, _C_FV).astype(BF)
    fv_ref[...] = proj(_C_FV, _C_FF).astype(BF)
    logf = _log_sigmoid(proj(_C_FF, _C_DU) + fb_ref[...])
    tri = (_iota2((tm, tm), 0) >= _iota2((tm, tm), 1)).astype(BF)

    @pl.when(pl.program_id(1) == 0)
    def _():
        carry_ref[...] = jnp.zeros_like(carry_ref)

    hi, mid, lo = _split3(logf)
    cum = _dot(tri, hi) + _dot(tri, mid) + _dot(tri, lo) + carry_ref[...]
    ff_ref[...] = cum
    carry_ref[...] = cum[tm - 1:tm, :]
    pick = (_iota2((SUBLANES, LANES), 0) == _iota2((SUBLANES, LANES), 1)).astype(BF)
    hi, mid, lo = _split3(cum)
    fft_ref[...] = _dot_nt(pick, hi) + _dot_nt(pick, mid) + _dot_nt(pick, lo)

    u = _gelu_tanh(proj(_C_DU, _C_DV))
    vn = _layer_norm(_gelu_tanh(proj(_C_DV, _C_END)), dlg_ref[...], dlb_ref[...])
    cc = GMLP_CHUNK
    causal = _iota2((cc, cc), 0) >= _iota2((cc, cc), 1)
    lane_group = _iota2((cc, GROUP_WIDTH), 1) >> HEAD_SHIFT
    w_cat = jnp.concatenate([jnp.where(causal, ws_ref[g], 0.0) for g in range(N_HEADS)],
                            axis=1).astype(BF)
    for j in range(tm // cc):
        vj = vn[j * cc:(j + 1) * cc, :]
        v_stack = jnp.concatenate([jnp.where(lane_group == g, vj, 0.0) for g in range(N_HEADS)],
                                  axis=0).astype(BF)
        mixed = bs_ref[...] + _dot(w_cat, v_stack)
        od_ref[j * cc:(j + 1) * cc, :] = (u[j * cc:(j + 1) * cc, :] * mixed).astype(BF)


def _place_heads(w, width, offsets):
    cols = []
    for hd in range(N_HEADS):
        blk = w[:, hd * width:(hd + 1) * width]
        off = offsets[hd]
        cols.append(jnp.pad(blk, ((0, 0), (off, HEAD_SLOT - off - width))))
    return jnp.concatenate(cols, axis=1)


def _mix_in_weights(mix_w_in, mla_w_uq, mla_w_ukv):
    d = mix_w_in.shape[0]
    gw = GROUP_WIDTH
    n_a = 4 * gw + MLA_Q_LORA + MLA_KV_LORA
    b_kr = mix_w_in[:, n_a:n_a + MLA_ROPE]
    o = n_a + MLA_ROPE
    c_qkv = mix_w_in[:, o:o + 3 * gw]
    c_f = mix_w_in[:, o + 3 * gw:o + 3 * gw + N_HEADS]
    d_uv = mix_w_in[:, o + 3 * gw + N_HEADS:]
    z = lambda n: jnp.zeros((d, n), mix_w_in.dtype)
    w_big = jnp.concatenate([
        mix_w_in[:, :n_a], z(MLA_NOPE), b_kr, z(HEAD_SLOT - MLA_NOPE - MLA_ROPE),
        c_qkv, c_f, z(LANES - N_HEADS), d_uv], axis=1).astype(BF)
    assert w_big.shape[1] == _C_END
    half = MLA_ROPE // 2
    qk = MLA_NOPE + MLA_ROPE
    uq = mla_w_uq.reshape(MLA_Q_LORA, N_HEADS, qk)
    zq = lambda n: jnp.zeros((MLA_Q_LORA, N_HEADS, n), mla_w_uq.dtype)
    uq_n = jnp.concatenate([uq, zq(HEAD_SLOT - qk)], axis=2)
    uq_s = jnp.concatenate([zq(MLA_NOPE), uq[:, :, MLA_NOPE + half:], uq[:, :, MLA_NOPE:MLA_NOPE + half],
                            zq(HEAD_SLOT - qk)], axis=2)
    w_uq2 = jnp.concatenate([uq_n.reshape(MLA_Q_LORA, -1), uq_s.reshape(MLA_Q_LORA, -1)], axis=1).astype(BF)
    ukv = mla_w_ukv.reshape(MLA_KV_LORA, N_HEADS, MLA_NOPE + HEAD_DIM)
    zero_off = (0,) * N_HEADS
    pair_off = tuple((hd % 2) * HEAD_DIM for hd in range(N_HEADS))
    k_n = _place_heads(ukv[:, :, :MLA_NOPE].reshape(MLA_KV_LORA, -1), MLA_NOPE, zero_off)
    v_p = _place_heads(ukv[:, :, MLA_NOPE:].reshape(MLA_KV_LORA, -1), HEAD_DIM, pair_off)
    w_ukv2 = jnp.concatenate([k_n, v_p], axis=1).astype(BF)
    return w_big, w_uq2, w_ukv2


def _rope_tables(s):
    half = MLA_ROPE // 2
    inv_freq = ROPE_THETA ** (-jnp.arange(half, dtype=F32) / half)
    ang = jnp.arange(s, dtype=F32)[:, None] * inv_freq[None, :]
    cos, sin = jnp.cos(ang), jnp.sin(ang)
    tail = HEAD_SLOT - MLA_NOPE - MLA_ROPE
    ct = jnp.concatenate([jnp.ones((s, MLA_NOPE), F32), cos, cos, jnp.zeros((s, tail), F32)], axis=1)
    st = jnp.concatenate([jnp.zeros((s, MLA_NOPE), F32), -sin, sin, jnp.zeros((s, tail), F32)], axis=1)
    return ct, st


def _mix_in(x, mod, w_big, w_uq2, w_ukv2, q_norm_g, kv_norm_g, ct, st, fox_b_f,
            gmlp_ln_g, gmlp_ln_b, gmlp_w_s, gmlp_b_s):
    b, s, d = x.shape
    tm = min(MIX_ROWS, s)
    nslot = N_HEADS * HEAD_SLOT
    gw = GROUP_WIDTH
    fb = jnp.pad(fox_b_f.reshape(1, N_HEADS), ((0, 0), (0, LANES - N_HEADS)))
    bias_tile = jnp.repeat(gmlp_b_s.T, HEAD_DIM, axis=1)
    full2 = lambda bi, i: (0, 0)
    tok = lambda n: pl.BlockSpec((None, tm, n), lambda bi, i: (bi, i, 0))
    shp = lambda n, dt: jax.ShapeDtypeStruct((b, s, n), dt)
    return pl.pallas_call(
        functools.partial(_mix_in_kernel, tm=tm, mla_scale=float((MLA_NOPE + MLA_ROPE) ** -0.5),
                          fox_scale=float(HEAD_DIM ** -0.5)),
        grid=(b, s // tm),
        in_specs=[
            tok(d),
            pl.BlockSpec((None, N_MOD, d), lambda bi, i: (bi, 0, 0)),
            _resident(w_big.shape, full2),
            _resident(w_uq2.shape, full2),
            _resident(w_ukv2.shape, full2),
            _resident((1, MLA_Q_LORA), full2),
            _resident((1, MLA_KV_LORA), full2),
            pl.BlockSpec((tm, HEAD_SLOT), lambda bi, i: (i, 0)),
            pl.BlockSpec((tm, HEAD_SLOT), lambda bi, i: (i, 0)),
            _resident((1, LANES), full2),
            _resident((1, gw), full2),
            _resident((1, gw), full2),
            _resident(gmlp_w_s.shape, lambda bi, i: (0, 0, 0)),
            _resident(bias_tile.shape, full2),
        ],
        out_specs=[tok(4 * gw), tok(nslot), tok(nslot), tok(nslot), tok(gw), tok(gw), tok(gw),
                   tok(LANES), pl.BlockSpec((None, SUBLANES, tm), lambda bi, i: (bi, 0, i)), tok(gw)],
        out_shape=[shp(4 * gw, F32), shp(nslot, BF), shp(nslot, BF), shp(nslot, BF),
                   shp(gw, BF), shp(gw, BF), shp(gw, BF), shp(LANES, F32),
                   jax.ShapeDtypeStruct((b, SUBLANES, s), F32), shp(gw, BF)],
        scratch_shapes=[pltpu.VMEM((1, LANES), F32)],
        compiler_params=_params(2),
        name="mixer_in",
    )(x, mod, w_big, w_uq2, w_ukv2, q_norm_g.reshape(1, -1), kv_norm_g.reshape(1, -1), ct, st, fb,
      gmlp_ln_g.reshape(1, gw), gmlp_ln_b.reshape(1, gw), gmlp_w_s, bias_tile)


def _hgrn_kernel(hg_ref, lbl_ref, ng_ref, o_ref, st_ref, oacc_ref, gk_ref, kv_ref, stb_ref,
                 *, layer, depth, tb):
    gw = GROUP_WIDTH
    ch = HGRN_CHUNK
    half = ch // 2
    n_ch = tb // ch

    @pl.when(pl.program_id(1) == 0)
    def _():
        st_ref[...] = jnp.zeros_like(st_ref)

    rows = [lbl_ref[j:j + 1, :] for j in range(depth)]
    top = functools.reduce(jnp.maximum, rows)
    ex = [jnp.exp(r - top) for r in rows]
    sm = [e / sum(ex) for e in ex]
    lb = sum(sm[:layer + 1]) - sm[0]
    lb_floor = jnp.maximum(lb, LB_FLOOR)

    z_q = hg_ref[:, 0:gw]
    z_f = hg_ref[:, gw:2 * gw]
    val = hg_ref[:, 2 * gw:3 * gw]
    z_g = hg_ref[:, 3 * gw:4 * gw]
    log_f = jnp.log(lb_floor + (1.0 - lb) * _sigmoid(z_f))
    kk = (1.0 - lb) * _sigmoid(-z_f) - (lb_floor - lb)
    qf = _silu(z_q)

    r_i = _iota2((tb, tb), 0)
    c_i = _iota2((tb, tb), 1)
    same_chunk = (r_i >> CHUNK_SHIFT) == (c_i >> CHUNK_SHIFT)
    tri_blk = (same_chunk & (r_i >= c_i)).astype(BF)
    ones_blk = same_chunk.astype(BF)
    hi, mid, lo = _split3(log_f)
    g_cum = _dot(tri_blk, hi) + _dot(tri_blk, mid) + _dot(tri_blk, lo)
    g_last = _dot(ones_blk, hi) + _dot(ones_blk, mid) + _dot(ones_blk, lo)
    q_dec = qf * jnp.exp(g_cum)
    k_end = kk * jnp.exp(g_last - g_cum)
    dec = jnp.exp(g_last)

    head_ones = ((_iota2((gw, gw), 0) >> HEAD_SHIFT) == (_iota2((gw, gw), 1) >> HEAD_SHIFT)).astype(BF)
    lane_head = _iota2((ch, gw), 1) >> HEAD_SHIFT
    head_lanes = [(lane_head == hd).astype(F32) for hd in range(N_HEADS)]
    pair_lanes = [(lane_head >> 1 == j).astype(F32) for j in range(2)]
    low_half = (_iota2((ch, LANES), 1) < HEAD_DIM).astype(F32)
    high_half = 1.0 - low_half

    gk_ref[0] = g_cum
    gk_ref[1] = kk
    t_half = _iota2((half, gw), 0)
    causal_half = [t_half >= s_ for s_ in range(half)]

    def row(ref_row):
        return jnp.broadcast_to(ref_row, (half, gw))

    for c in range(n_ch):
        r0 = c * ch
        v_c = val[r0:r0 + ch, :]
        v_lo, v_hi = v_c[:, :LANES], v_c[:, LANES:]
        v_stack = jnp.concatenate([v_lo * low_half, v_lo * high_half, v_hi * low_half, v_hi * high_half],
                                  axis=0).astype(BF)
        ke_c = k_end[r0:r0 + ch, :]
        k_stack = jnp.concatenate([ke_c * head_lanes[hd] for hd in range(N_HEADS)],
                                  axis=0).astype(BF)
        kv_ref[c] = _dot_tn(v_stack, k_stack)
    for c in range(n_ch):
        st_c = st_ref[...]
        stb_ref[c] = st_c.astype(BF)
        st_ref[...] = st_c * dec[c * ch:c * ch + 1, :] + kv_ref[c]
    for c in range(n_ch):
        r0 = c * ch
        qd_c = q_dec[r0:r0 + ch, :]
        q_pairs = jnp.concatenate([qd_c * pair_lanes[0], qd_c * pair_lanes[1]], axis=0).astype(BF)
        inter = _dot_nt(q_pairs, stb_ref[c])
        oacc_ref[r0:r0 + ch, :] = jnp.concatenate([inter[:ch, :], inter[ch:, :]], axis=1)
    for c in range(n_ch):
        r0 = c * ch
        g_a, g_b = g_cum[r0:r0 + half, :], g_cum[r0 + half:r0 + ch, :]
        q_a, q_b = qf[r0:r0 + half, :], qf[r0 + half:r0 + ch, :]
        early, late = [], []
        for s_ in range(ch):
            gs = row(gk_ref[0, r0 + s_:r0 + s_ + 1, :])
            ks = row(gk_ref[1, r0 + s_:r0 + s_ + 1, :])
            if s_ < half:
                pa = jnp.where(causal_half[s_], jnp.exp(g_a - gs), 0.0) * (q_a * ks)
                pb = jnp.exp(g_b - gs) * (q_b * ks)
                early += [pa, pb]
            else:
                late.append(jnp.where(causal_half[s_ - half], jnp.exp(g_b - gs), 0.0) * (q_b * ks))
        p_rows = jnp.concatenate(early + late, axis=0).astype(BF)
        scores = _dot(p_rows, head_ones)
        o_a = o_b = None
        for s_ in range(ch):
            vs = row(hg_ref[r0 + s_:r0 + s_ + 1, 2 * gw:3 * gw])
            if s_ < half:
                ta = scores[s_ * ch:s_ * ch + half, :] * vs
                tb_ = scores[s_ * ch + half:(s_ + 1) * ch, :] * vs
                o_a = ta if o_a is None else o_a + ta
            else:
                base = half * ch + (s_ - half) * half
                tb_ = scores[base:base + half, :] * vs
            o_b = tb_ if o_b is None else o_b + tb_
        oacc_ref[r0:r0 + half, :] += o_a
        oacc_ref[r0 + half:r0 + ch, :] += o_b

    o = oacc_ref[...]
    h2, l2, _ = _split3(o * o)
    ms = (_dot(h2, head_ones) + _dot(l2, head_ones)) * (1.0 / HEAD_DIM)
    o = o * lax.rsqrt(ms + RMS_EPS) * ng_ref[...]
    o_ref[...] = (o * _silu(z_g)).astype(BF)


def _hgrn(hg, lb_logits, norm_g, *, layer):
    b, s, _ = hg.shape
    depth = lb_logits.shape[0]
    gw = GROUP_WIDTH
    tb = min(HGRN_ROWS, s)
    return pl.pallas_call(
        functools.partial(_hgrn_kernel, layer=layer, depth=depth, tb=tb),
        grid=(b, s // tb),
        in_specs=[
            pl.BlockSpec((None, tb, 4 * gw), lambda bi, i: (bi, i, 0)),
            pl.BlockSpec((depth, gw), lambda bi, i: (0, 0)),
            pl.BlockSpec((1, gw), lambda bi, i: (0, 0)),
        ],
        out_specs=pl.BlockSpec((None, tb, gw), lambda bi, i: (bi, i, 0)),
        out_shape=jax.ShapeDtypeStruct((b, s, gw), BF),
        scratch_shapes=[pltpu.VMEM((2 * HEAD_DIM, gw), F32),
                        pltpu.VMEM((tb, gw), F32),
                        pltpu.VMEM((2, tb, gw), F32),
                        pltpu.VMEM((tb // HGRN_CHUNK, 2 * HEAD_DIM, gw), F32),
                        pltpu.VMEM((tb // HGRN_CHUNK, 2 * HEAD_DIM, gw), BF)],
        compiler_params=_params(2),
        name="hgrn2",
    )(hg, lb_logits, norm_g.reshape(1, gw))


def _attn_kernel(*refs, s_len, tq, has_bias, packed):
    if has_bias:
        q_ref, k_ref, v_ref, bq_ref, bk_ref, o_ref, s_ref = refs
    else:
        q_ref, k_ref, v_ref, o_ref, s_ref = refs
    pair = pl.program_id(1)
    n_q = s_len // tq
    n_g = tq // LANES
    causal = _iota2((tq, tq), 0) >= _iota2((tq, tq), 1)
    if packed:
        lane_half = _iota2((tq, LANES), 1) >> HEAD_SHIFT
        keep = [(lane_half == hh).astype(F32).astype(BF) for hh in range(2)]
    for qi in range(n_q):
        rows = slice(qi * tq, (qi + 1) * tq)
        out = None
        for hh in range(2):
            lanes = slice(0, LANES) if packed else slice(hh * HEAD_SLOT, (hh + 1) * HEAD_SLOT)
            stage = s_ref.at[2 * (qi % 2) + hh]
            q = q_ref[rows, lanes]
            if packed:
                q = q * keep[hh]
            if has_bias:
                head = 2 * pair + hh
                lane_id = _iota2((tq, LANES), 1)
                bq = jnp.sum(jnp.where(lane_id == head, bq_ref[rows, :], 0.0), axis=1, keepdims=True)
            m_part = None
            for j in range(qi + 1):
                cols = slice(j * tq, (j + 1) * tq)
                sj = _dot_nt(q, k_ref[cols, lanes])
                if has_bias:
                    row_id = _iota2((SUBLANES, tq), 0)
                    bk = jnp.sum(jnp.where(row_id == head, bk_ref[:, cols], 0.0), axis=0, keepdims=True)
                    sj = sj + (bq - bk)
                if j == qi:
                    sj = jnp.where(causal, sj, NEG_BIG)
                stage[:, cols] = sj
                for g in range(n_g):
                    blk = sj[:, g * LANES:(g + 1) * LANES]
                    m_part = blk if m_part is None else jnp.maximum(m_part, blk)
            m_b = jnp.broadcast_to(jnp.max(m_part, axis=1, keepdims=True), (tq, LANES))
            l_part = jnp.zeros((tq, LANES), F32)
            acc = jnp.zeros((tq, LANES), F32)
            for j in range(qi + 1):
                ps = []
                for g in range(n_g):
                    c0 = j * tq + g * LANES
                    p = jnp.exp(stage[:, c0:c0 + LANES] - m_b)
                    l_part = l_part + p
                    ps.append(p.astype(BF))
                vj = v_ref[j * tq:(j + 1) * tq, lanes]
                if packed:
                    vj = vj * keep[hh]
                acc = acc + _dot(jnp.concatenate(ps, axis=1), vj)
            o_h = acc * (1.0 / jnp.sum(l_part, axis=1, keepdims=True))
            out = o_h if out is None else out + o_h
        o_ref[rows, :] = out.astype(BF)


def _attention(q, k, v, bias_q=None, bias_k=None, *, packed):
    b, s, _ = q.shape
    tq = min(ATT_Q_ROWS, s)
    has_bias = bias_q is not None
    pairw = LANES if packed else 2 * HEAD_SLOT
    seq = lambda n: pl.BlockSpec((None, s, n), lambda bi, p: (bi, 0, p))
    in_specs = [seq(pairw), seq(pairw), seq(pairw)]
    args = [q, k, v]
    if has_bias:
        in_specs += [pl.BlockSpec((None, s, LANES), lambda bi, p: (bi, 0, 0)),
                     pl.BlockSpec((None, SUBLANES, s), lambda bi, p: (bi, 0, 0))]
        args += [bias_q, bias_k]
    return pl.pallas_call(
        functools.partial(_attn_kernel, s_len=s, tq=tq, has_bias=has_bias, packed=packed),
        grid=(b, N_HEADS // 2),
        in_specs=in_specs,
        out_specs=seq(LANES),
        out_shape=jax.ShapeDtypeStruct((b, s, (N_HEADS // 2) * LANES), BF),
        scratch_shapes=[pltpu.VMEM((4, tq, s), F32)],
        compiler_params=_params(2),
        name="fox_attention" if has_bias else "mla_attention",
    )(*args)


def kernel(x, c, ada_w, ada_b, ln_g, ln_b, ffn1_w_in, ffn1_w_out, ffn2_w_in, ffn2_w_out, mix_w_in, mix_w_out, hgrn_lb_logits, hgrn_norm_g, mla_q_norm_g, mla_kv_norm_g, mla_w_uq, mla_w_ukv, fox_b_f, gmlp_ln_g, gmlp_ln_b, gmlp_w_s, gmlp_b_s):
    depth = ada_w.shape[0]
    b, s, d = x.shape
    assert d == N_HEADS * GROUP_WIDTH and s % GMLP_CHUNK == 0
    alpha = float((2 * depth) ** 0.25)
    mod_all = _modulation(c, ada_w, ada_b).reshape(depth, b, N_MOD, d)
    ct, st = _rope_tables(s)
    w1_in, w1_out = ffn1_w_in.astype(BF), ffn1_w_out.astype(BF)
    w2_in, w2_out = ffn2_w_in.astype(BF), ffn2_w_out.astype(BF)
    w_mix_out = mix_w_out.astype(BF)
    for l in range(depth):
        mod = mod_all[l]
        x = _ffn_sublayer(x, mod, w1_in, w1_out, ln_g, ln_b, l, mod_row=0, ln_row=0, alpha=alpha)
        w_big, w_uq2, w_ukv2 = _mix_in_weights(mix_w_in[l], mla_w_uq[l], mla_w_ukv[l])
        hg, mq, mk, mv, fq, fk, fv, ff, fft, od = _mix_in(
            x, mod, w_big, w_uq2, w_ukv2, mla_q_norm_g[l], mla_kv_norm_g[l], ct, st, fox_b_f[l],
            gmlp_ln_g[l], gmlp_ln_b[l], gmlp_w_s[l], gmlp_b_s[l])
        oa = _hgrn(hg, hgrn_lb_logits, hgrn_norm_g[l], layer=l)
        ob = _attention(mq, mk, mv, packed=False)
        oc = _attention(fq, fk, fv, ff, fft, packed=True)
        x = _ffn_sublayer(x, mod, w2_in, w2_out, ln_g, ln_b, l, mod_row=6, ln_row=2, alpha=alpha,
                          mix=(oa, ob, oc, od, w_mix_out))
    return x
```

```python
import functools

import jax
import jax.numpy as jnp
from jax import lax
from jax.experimental import pallas as pl
from jax.experimental.pallas import tpu as pltpu

BF = jnp.bfloat16
F32 = jnp.float32

N_HEADS = 4
GROUP_WIDTH = 256
HEAD_DIM = 64
HEAD_SHIFT = 6
HGRN_CHUNK = 16
CHUNK_SHIFT = 4
MLA_Q_LORA = 256
MLA_KV_LORA = 128
MLA_NOPE = 64
MLA_ROPE = 32
ROPE_THETA = 10000.0
GMLP_CHUNK = 128
N_MOD = 9
LN_EPS = 1e-5
RMS_EPS = 1e-6
LB_FLOOR = 1e-30
NEG_BIG = -1e30

LANES = 128
SUBLANES = 8
HEAD_SLOT = 128
ROW_GROUP = 16
VMEM_LIMIT_BYTES = 56 * 1024 * 1024

FFN_ROWS = 512
FFN_CHUNK = 256
MIX_ROWS = 512
HGRN_ROWS = 256
ATT_Q_ROWS = 256


def _dot(a, b):
    return jnp.dot(a, b, preferred_element_type=F32)


def _dot_nt(a, b):
    return lax.dot_general(a, b, (((1,), (1,)), ((), ())), preferred_element_type=F32)


def _dot_tn(a, b):
    return lax.dot_general(a, b, (((0,), (0,)), ((), ())), preferred_element_type=F32)


def _sigmoid(x):
    return jax.nn.sigmoid(x)


def _silu(x):
    return x * _sigmoid(x)


def _log_sigmoid(x):
    return jnp.minimum(x, 0.0) - jnp.log1p(jnp.exp(-jnp.abs(x)))


def _gelu_tanh(x):
    c = 0.7978845608028654
    return 0.5 * x * (1.0 + jnp.tanh(c * (x + 0.044715 * (x * x * x))))


def _layer_norm(r, g, b):
    mu = jnp.mean(r, axis=-1, keepdims=True)
    d = r - mu
    var = jnp.mean(d * d, axis=-1, keepdims=True)
    return d * lax.rsqrt(var + LN_EPS) * g + b


def _rms_norm(r, g):
    return r * lax.rsqrt(jnp.mean(r * r, axis=-1, keepdims=True) + RMS_EPS) * g


def _split3(x):
    hi = x.astype(BF)
    r1 = x - hi.astype(F32)
    mid = r1.astype(BF)
    lo = (r1 - mid.astype(F32)).astype(BF)
    return hi, mid, lo


def _iota2(shape, dim):
    return lax.broadcasted_iota(jnp.int32, shape, dim)


def _resident(shape, index_map):
    return pl.BlockSpec(shape, index_map, pipeline_mode=pl.Buffered(1))


def _params(n_axes):
    return pltpu.CompilerParams(dimension_semantics=("arbitrary",) * n_axes,
                                vmem_limit_bytes=VMEM_LIMIT_BYTES)


def _mod_kernel(c_ref, w_ref, b_ref, o_ref):
    c = c_ref[...]
    o_ref[...] = _dot(_silu(c).astype(BF), w_ref[...].astype(BF)) + b_ref[...]


def _modulation(c, ada_w, ada_b):
    depth, d, n = ada_w.shape
    b = c.shape[0]
    tn = 1152 if n % 1152 == 0 else n
    return pl.pallas_call(
        _mod_kernel,
        grid=(depth, n // tn),
        in_specs=[
            pl.BlockSpec((b, d), lambda l, j: (0, 0)),
            pl.BlockSpec((None, d, tn), lambda l, j: (l, 0, j)),
            pl.BlockSpec((None, 1, tn), lambda l, j: (l, 0, j)),
        ],
        out_specs=pl.BlockSpec((None, b, tn), lambda l, j: (l, 0, j)),
        out_shape=jax.ShapeDtypeStruct((depth, b, n), F32),
        compiler_params=_params(2),
        name="adaln_modulation",
    )(c, ada_w, ada_b.reshape(depth, 1, n))


def _ffn_kernel(*refs, mod_row, ln_row, dff, ck, alpha, with_mix):
    if with_mix:
        (x_ref, mod_ref, oa_ref, ob_ref, oc_ref, od_ref, wm_ref, wi_ref, wo_ref, lng_ref, lnb_ref,
         o_ref, acc_ref, h_ref, x1_ref) = refs
    else:
        x_ref, mod_ref, wi_ref, wo_ref, lng_ref, lnb_ref, o_ref, acc_ref, h_ref = refs
    tm, d = x_ref.shape
    groups = [slice(r, r + ROW_GROUP) for r in range(0, tm, ROW_GROUP)]
    rep = lambda row: jnp.broadcast_to(row, (ROW_GROUP, d))
    sh = rep(mod_ref[mod_row:mod_row + 1, :])
    sc1 = rep(1.0 + mod_ref[mod_row + 1:mod_row + 2, :])
    gt_half = rep(0.5 * (1.0 + mod_ref[mod_row + 2:mod_row + 3, :]))
    if with_mix:
        gw = GROUP_WIDTH
        y = (_dot(oa_ref[...], wm_ref[0:gw, :]) + _dot(ob_ref[...], wm_ref[gw:2 * gw, :])
             + _dot(oc_ref[...], wm_ref[2 * gw:3 * gw, :]) + _dot(od_ref[...], wm_ref[3 * gw:4 * gw, :]))
        g2 = rep(1.0 + mod_ref[5:6, :])
        lg0, lb0 = rep(lng_ref[ln_row - 1:ln_row, :]), rep(lnb_ref[ln_row - 1:ln_row, :])
        for rows in groups:
            x1 = _layer_norm(alpha * x_ref[rows, :] + g2 * y[rows, :], lg0, lb0)
            x1_ref[rows, :] = x1
            h_ref[rows, :] = (x1 * sc1 + sh).astype(BF)
        xin_ref = x1_ref
    else:
        for rows in groups:
            h_ref[rows, :] = (x_ref[rows, :] * sc1 + sh).astype(BF)
        xin_ref = x_ref
    for c in range(dff // ck):
        gate = _dot(h_ref[...], wi_ref[:, c * ck:(c + 1) * ck])
        up = _dot(h_ref[...], wi_ref[:, dff + c * ck:dff + (c + 1) * ck])
        acc_ref[:, c * ck:(c + 1) * ck] = (_silu(gate) * up).astype(BF)
    y = _dot(acc_ref[...], wo_ref[...])
    lg, lb = rep(lng_ref[ln_row:ln_row + 1, :]), rep(lnb_ref[ln_row:ln_row + 1, :])
    for rows in groups:
        o_ref[rows, :] = _layer_norm(alpha * xin_ref[rows, :] + gt_half * y[rows, :], lg, lb)


def _ffn_sublayer(x, mod, w_in, w_out, ln_g, ln_b, layer, *, mod_row, ln_row, alpha, mix=None):
    b, s, d = x.shape
    dff = w_out.shape[1]
    ck = FFN_CHUNK if dff % FFN_CHUNK == 0 else dff
    tm = min(FFN_ROWS, s)
    tok = lambda n: pl.BlockSpec((None, tm, n), lambda bi, i: (bi, i, 0))
    lay = lambda bi, i: (layer, 0, 0)
    in_specs = [tok(d), pl.BlockSpec((None, N_MOD, d), lambda bi, i: (bi, 0, 0))]
    args = [x, mod]
    if mix is not None:
        oa, ob, oc, od, w_mix = mix
        gw = GROUP_WIDTH
        in_specs += [tok(gw), tok(gw), tok(gw), tok(gw), _resident((None, d, d), lay)]
        args += [oa, ob, oc, od, w_mix]
    in_specs += [_resident((None, d, 2 * dff), lay), _resident((None, dff, d), lay),
                 _resident((None,) + ln_g.shape[1:], lay), _resident((None,) + ln_b.shape[1:], lay)]
    args += [w_in, w_out, ln_g, ln_b]
    return pl.pallas_call(
        functools.partial(_ffn_kernel, mod_row=mod_row, ln_row=ln_row, dff=dff, ck=ck, alpha=alpha,
                          with_mix=mix is not None),
        grid=(b, s // tm),
        in_specs=in_specs,
        out_specs=tok(d),
        out_shape=jax.ShapeDtypeStruct((b, s, d), F32),
        scratch_shapes=[pltpu.VMEM((tm, dff), BF), pltpu.VMEM((tm, d), BF)]
        + ([pltpu.VMEM((tm, d), F32)] if mix is not None else []),
        compiler_params=_params(2),
        name="mix_out_ffn" if mix is not None else "ffn_sublayer",
    )(*args)


_C_CQ = 0
_C_CKV = 256
_C_KRF = 384
_C_DU = 512
_C_DV = 768
_C_FOX = 1024
_C_HG = 1792
_C_END = 2816
CUM_ROWS = 256


def _mix_in_kernel(x_ref, mod_ref, w_ref, wuq_ref, wukv_ref, qg_ref, kvg_ref, ct_ref, st_ref,
                   fb_ref, dlg_ref, dlb_ref, ws_ref, bs_ref,
                   hg_ref, mq_ref, mk_ref, mv_ref, fq_ref, fk_ref, fv_ref, fft_ref, od_ref,
                   carry_ref, h_ref, *, tm, mla_scale, fox_scale):
    d = x_ref.shape[1]
    sh = jnp.broadcast_to(mod_ref[3:4, :], (ROW_GROUP, d))
    sc1 = jnp.broadcast_to(1.0 + mod_ref[4:5, :], (ROW_GROUP, d))
    for r in range(0, tm, ROW_GROUP):
        h_ref[r:r + ROW_GROUP, :] = (x_ref[r:r + ROW_GROUP, :] * sc1 + sh).astype(BF)
    h = h_ref[...]
    gw = GROUP_WIDTH

    @pl.when(pl.program_id(1) == 0)
    def _():
        carry_ref[...] = jnp.zeros_like(carry_ref)

    s1 = _dot(h, w_ref[:, _C_CQ:_C_FOX])
    c_q = s1[:, _C_CQ:_C_CKV]
    c_kv = s1[:, _C_CKV:_C_KRF]
    krf = s1[:, _C_KRF:_C_DU]
    d_u = s1[:, _C_DU:_C_DV]
    d_v = s1[:, _C_DV:_C_FOX]

    fox = _dot(h, w_ref[:, _C_FOX:_C_HG])
    fq_ref[...] = (fox[:, 0:gw] * fox_scale).astype(BF)
    fk_ref[...] = fox[:, gw:2 * gw].T.astype(BF)
    fv_ref[...] = fox[:, 2 * gw:3 * gw].astype(BF)
    nq = _rms_norm(c_q, qg_ref[...]).astype(BF)
    nkv = _rms_norm(c_kv, kvg_ref[...]).astype(BF)
    logf = _log_sigmoid(krf + fb_ref[...])
    u = _gelu_tanh(d_u)
    vn = _layer_norm(_gelu_tanh(d_v), dlg_ref[...], dlb_ref[...])

    q2 = _dot(nq, wuq_ref[...])
    kv2 = _dot(nkv, wukv_ref[...])
    cr = min(CUM_ROWS, tm)
    tri = (_iota2((cr, cr), 0) >= _iota2((cr, cr), 1)).astype(BF)
    pick = (_iota2((SUBLANES, LANES), 0) == _iota2((SUBLANES, LANES), 1)).astype(BF)
    run = carry_ref[...]
    for j in range(tm // cr):
        f_hi, f_mid, f_lo = _split3(logf[j * cr:(j + 1) * cr, :])
        cum = _dot(tri, f_hi) + _dot(tri, f_mid) + _dot(tri, f_lo) + run
        run = cum[cr - 1:cr, :]
        c_hi, c_mid, c_lo = _split3(cum)
        fft_ref[:, j * cr:(j + 1) * cr] = _dot_nt(pick, c_hi) + _dot_nt(pick, c_mid) + _dot_nt(pick, c_lo)
    carry_ref[...] = run
    cc = GMLP_CHUNK
    causal = _iota2((cc, cc), 0) >= _iota2((cc, cc), 1)
    lane_group = _iota2((cc, gw), 1) >> HEAD_SHIFT
    w_cat = jnp.concatenate([jnp.where(causal, ws_ref[g], 0.0) for g in range(N_HEADS)],
                            axis=1).astype(BF)
    for j in range(tm // cc):
        vj = vn[j * cc:(j + 1) * cc, :]
        v_stack = jnp.concatenate([jnp.where(lane_group == g, vj, 0.0) for g in range(N_HEADS)],
                                  axis=0).astype(BF)
        mixed = bs_ref[...] + _dot(w_cat, v_stack)
        od_ref[j * cc:(j + 1) * cc, :] = (u[j * cc:(j + 1) * cc, :] * mixed).astype(BF)

    hg_ref[...] = _dot(h, w_ref[:, _C_HG:_C_END])
    ct = ct_ref[...]
    st = st_ref[...]
    ct4 = jnp.concatenate([ct] * N_HEADS, axis=1)
    st4 = jnp.concatenate([st] * N_HEADS, axis=1)
    nslot = N_HEADS * HEAD_SLOT
    q_rot = q2[:, :nslot] * ct4 + q2[:, nslot:] * st4
    mq_ref[...] = (q_rot * mla_scale).astype(BF)
    half = MLA_ROPE // 2
    lane = _iota2((tm, HEAD_SLOT), 1)
    kr = jnp.where(lane >= MLA_NOPE, krf, 0.0)
    kr_swap = jnp.where(lane < MLA_NOPE + half, pltpu.roll(kr, HEAD_SLOT - half, 1), pltpu.roll(kr, half, 1))
    kr_rot = kr * ct + kr_swap * st
    mk_ref[...] = (kv2[:, :nslot] + jnp.concatenate([kr_rot] * N_HEADS, axis=1)).T.astype(BF)
    mv_ref[...] = kv2[:, nslot:].astype(BF)


def _place_heads(w, width, offsets):
    cols = []
    for hd in range(N_HEADS):
        blk = w[:, hd * width:(hd + 1) * width]
        off = offsets[hd]
        cols.append(jnp.pad(blk, ((0, 0), (off, HEAD_SLOT - off - width))))
    return jnp.concatenate(cols, axis=1)


def _mix_in_weights(mix_w_in, mla_w_uq, mla_w_ukv):
    d = mix_w_in.shape[0]
    gw = GROUP_WIDTH
    w = mix_w_in.astype(BF)
    n_hg = 4 * gw
    n_a = n_hg + MLA_Q_LORA + MLA_KV_LORA
    b_kr = w[:, n_a:n_a + MLA_ROPE]
    o = n_a + MLA_ROPE
    c_qkv = w[:, o:o + 3 * gw]
    c_f = w[:, o + 3 * gw:o + 3 * gw + N_HEADS]
    d_uv = w[:, o + 3 * gw + N_HEADS:]
    z = lambda n: jnp.zeros((d, n), BF)
    w_big = jnp.concatenate([
        w[:, n_hg:n_a], c_f, z(MLA_NOPE - N_HEADS), b_kr, z(HEAD_SLOT - MLA_NOPE - MLA_ROPE),
        d_uv, c_qkv, w[:, :n_hg]], axis=1)
    assert w_big.shape[1] == _C_END
    half = MLA_ROPE // 2
    qk = MLA_NOPE + MLA_ROPE
    uq = mla_w_uq.reshape(MLA_Q_LORA, N_HEADS, qk)
    zq = lambda n: jnp.zeros((MLA_Q_LORA, N_HEADS, n), mla_w_uq.dtype)
    uq_n = jnp.concatenate([uq, zq(HEAD_SLOT - qk)], axis=2)
    uq_s = jnp.concatenate([zq(MLA_NOPE), uq[:, :, MLA_NOPE + half:], uq[:, :, MLA_NOPE:MLA_NOPE + half],
                            zq(HEAD_SLOT - qk)], axis=2)
    w_uq2 = jnp.concatenate([uq_n.reshape(MLA_Q_LORA, -1), uq_s.reshape(MLA_Q_LORA, -1)], axis=1).astype(BF)
    ukv = mla_w_ukv.reshape(MLA_KV_LORA, N_HEADS, MLA_NOPE + HEAD_DIM)
    zero_off = (0,) * N_HEADS
    pair_off = tuple((hd % 2) * HEAD_DIM for hd in range(N_HEADS))
    k_n = _place_heads(ukv[:, :, :MLA_NOPE].reshape(MLA_KV_LORA, -1), MLA_NOPE, zero_off)
    v_p = _place_heads(ukv[:, :, MLA_NOPE:].reshape(MLA_KV_LORA, -1), HEAD_DIM, pair_off)
    w_ukv2 = jnp.concatenate([k_n, v_p], axis=1).astype(BF)
    return w_big, w_uq2, w_ukv2


def _rope_tables(s):
    half = MLA_ROPE // 2
    inv_freq = ROPE_THETA ** (-jnp.arange(half, dtype=F32) / half)
    ang = jnp.arange(s, dtype=F32)[:, None] * inv_freq[None, :]
    cos, sin = jnp.cos(ang), jnp.sin(ang)
    tail = HEAD_SLOT - MLA_NOPE - MLA_ROPE
    ct = jnp.concatenate([jnp.ones((s, MLA_NOPE), F32), cos, cos, jnp.zeros((s, tail), F32)], axis=1)
    st = jnp.concatenate([jnp.zeros((s, MLA_NOPE), F32), -sin, sin, jnp.zeros((s, tail), F32)], axis=1)
    return ct, st


def _mix_in(x, mod, w_big, w_uq2, w_ukv2, q_norm_g, kv_norm_g, ct, st, fox_b_f,
            gmlp_ln_g, gmlp_ln_b, gmlp_w_s, gmlp_b_s):
    b, s, d = x.shape
    tm = min(MIX_ROWS, s)
    nslot = N_HEADS * HEAD_SLOT
    gw = GROUP_WIDTH
    fb = jnp.pad(fox_b_f.reshape(1, N_HEADS), ((0, 0), (0, LANES - N_HEADS)))
    bias_tile = jnp.repeat(gmlp_b_s.T, HEAD_DIM, axis=1)
    full2 = lambda bi, i: (0, 0)
    tok = lambda n: pl.BlockSpec((None, tm, n), lambda bi, i: (bi, i, 0))
    shp = lambda n, dt: jax.ShapeDtypeStruct((b, s, n), dt)
    tok_t = lambda n: pl.BlockSpec((None, n, tm), lambda bi, i: (bi, 0, i))
    shp_t = lambda n: jax.ShapeDtypeStruct((b, n, s), BF)
    return pl.pallas_call(
        functools.partial(_mix_in_kernel, tm=tm, mla_scale=float((MLA_NOPE + MLA_ROPE) ** -0.5),
                          fox_scale=float(HEAD_DIM ** -0.5)),
        grid=(b, s // tm),
        in_specs=[
            tok(d),
            pl.BlockSpec((None, N_MOD, d), lambda bi, i: (bi, 0, 0)),
            _resident(w_big.shape, full2),
            _resident(w_uq2.shape, full2),
            _resident(w_ukv2.shape, full2),
            _resident((1, MLA_Q_LORA), full2),
            _resident((1, MLA_KV_LORA), full2),
            pl.BlockSpec((tm, HEAD_SLOT), lambda bi, i: (i, 0)),
            pl.BlockSpec((tm, HEAD_SLOT), lambda bi, i: (i, 0)),
            _resident((1, LANES), full2),
            _resident((1, gw), full2),
            _resident((1, gw), full2),
            _resident(gmlp_w_s.shape, lambda bi, i: (0, 0, 0)),
            _resident(bias_tile.shape, full2),
        ],
        out_specs=[tok(4 * gw), tok(nslot), tok_t(nslot), tok(nslot), tok(gw), tok_t(gw), tok(gw),
                   pl.BlockSpec((None, SUBLANES, tm), lambda bi, i: (bi, 0, i)), tok(gw)],
        out_shape=[shp(4 * gw, F32), shp(nslot, BF), shp_t(nslot), shp(nslot, BF),
                   shp(gw, BF), shp_t(gw), shp(gw, BF),
                   jax.ShapeDtypeStruct((b, SUBLANES, s), F32), shp(gw, BF)],
        scratch_shapes=[pltpu.VMEM((1, LANES), F32), pltpu.VMEM((tm, d), BF)],
        compiler_params=_params(2),
        name="mixer_in",
    )(x, mod, w_big, w_uq2, w_ukv2, q_norm_g.reshape(1, -1), kv_norm_g.reshape(1, -1), ct, st, fb,
      gmlp_ln_g.reshape(1, gw), gmlp_ln_b.reshape(1, gw), gmlp_w_s, bias_tile)


def _hgrn_kernel(hg_ref, lbl_ref, ng_ref, o_ref, st_ref, oacc_ref, gk_ref, kv_ref, stb_ref,
                 *, layer, depth, tb):
    gw = GROUP_WIDTH
    ch = HGRN_CHUNK
    half = ch // 2
    n_ch = tb // ch

    @pl.when(pl.program_id(1) == 0)
    def _():
        st_ref[...] = jnp.zeros_like(st_ref)

    rows = [lbl_ref[j:j + 1, :] for j in range(depth)]
    top = functools.reduce(jnp.maximum, rows)
    ex = [jnp.exp(r - top) for r in rows]
    sm = [e / sum(ex) for e in ex]
    lb = sum(sm[:layer + 1]) - sm[0]
    lb_floor = jnp.maximum(lb, LB_FLOOR)

    z_q = hg_ref[:, 0:gw]
    z_f = hg_ref[:, gw:2 * gw]
    val = hg_ref[:, 2 * gw:3 * gw]
    z_g = hg_ref[:, 3 * gw:4 * gw]
    log_f = jnp.log(lb_floor + (1.0 - lb) * _sigmoid(z_f))
    kk = (1.0 - lb) * _sigmoid(-z_f) - (lb_floor - lb)
    qf = _silu(z_q)

    r_i = _iota2((tb, tb), 0)
    c_i = _iota2((tb, tb), 1)
    same_chunk = (r_i >> CHUNK_SHIFT) == (c_i >> CHUNK_SHIFT)
    tri_blk = (same_chunk & (r_i >= c_i)).astype(BF)
    ones_blk = same_chunk.astype(BF)
    hi, mid, lo = _split3(log_f)
    g_cum = _dot(tri_blk, hi) + _dot(tri_blk, mid) + _dot(tri_blk, lo)
    g_last = _dot(ones_blk, hi) + _dot(ones_blk, mid) + _dot(ones_blk, lo)
    q_dec = qf * jnp.exp(g_cum)
    k_end = kk * jnp.exp(g_last - g_cum)
    dec = jnp.exp(g_last)

    head_ones = ((_iota2((gw, gw), 0) >> HEAD_SHIFT) == (_iota2((gw, gw), 1) >> HEAD_SHIFT)).astype(BF)
    lane_head = _iota2((ch, gw), 1) >> HEAD_SHIFT
    head_lanes = [(lane_head == hd).astype(F32) for hd in range(N_HEADS)]
    pair_lanes = [(lane_head >> 1 == j).astype(F32) for j in range(2)]
    low_half = (_iota2((ch, LANES), 1) < HEAD_DIM).astype(F32)
    high_half = 1.0 - low_half

    gk_ref[0] = g_cum
    gk_ref[1] = kk
    t_half = _iota2((half, gw), 0)
    causal_half = [t_half >= s_ for s_ in range(half)]

    def row(ref_row):
        return jnp.broadcast_to(ref_row, (half, gw))

    for c in range(n_ch):
        r0 = c * ch
        v_c = val[r0:r0 + ch, :]
        v_lo, v_hi = v_c[:, :LANES], v_c[:, LANES:]
        v_stack = jnp.concatenate([v_lo * low_half, v_lo * high_half, v_hi * low_half, v_hi * high_half],
                                  axis=0).astype(BF)
        ke_c = k_end[r0:r0 + ch, :]
        k_stack = jnp.concatenate([ke_c * head_lanes[hd] for hd in range(N_HEADS)],
                                  axis=0).astype(BF)
        kv_ref[c] = _dot_tn(v_stack, k_stack)
    for c in range(n_ch):
        st_c = st_ref[...]
        stb_ref[c] = st_c.astype(BF)
        st_ref[...] = st_c * dec[c * ch:c * ch + 1, :] + kv_ref[c]
    for c in range(n_ch):
        r0 = c * ch
        qd_c = q_dec[r0:r0 + ch, :]
        q_pairs = jnp.concatenate([qd_c * pair_lanes[0], qd_c * pair_lanes[1]], axis=0).astype(BF)
        inter = _dot_nt(q_pairs, stb_ref[c])
        oacc_ref[r0:r0 + ch, :] = jnp.concatenate([inter[:ch, :], inter[ch:, :]], axis=1)
    for c in range(n_ch):
        r0 = c * ch
        g_a, g_b = g_cum[r0:r0 + half, :], g_cum[r0 + half:r0 + ch, :]
        q_a, q_b = qf[r0:r0 + half, :], qf[r0 + half:r0 + ch, :]
        early, late = [], []
        for s_ in range(ch):
            gs = row(gk_ref[0, r0 + s_:r0 + s_ + 1, :])
            ks = row(gk_ref[1, r0 + s_:r0 + s_ + 1, :])
            if s_ < half:
                pa = jnp.where(causal_half[s_], jnp.exp(g_a - gs), 0.0) * (q_a * ks)
                pb = jnp.exp(g_b - gs) * (q_b * ks)
                early += [pa, pb]
            else:
                late.append(jnp.where(causal_half[s_ - half], jnp.exp(g_b - gs), 0.0) * (q_b * ks))
        p_rows = jnp.concatenate(early + late, axis=0).astype(BF)
        scores = _dot(p_rows, head_ones)
        o_a = o_b = None
        for s_ in range(ch):
            vs = row(hg_ref[r0 + s_:r0 + s_ + 1, 2 * gw:3 * gw])
            if s_ < half:
                ta = scores[s_ * ch:s_ * ch + half, :] * vs
                tb_ = scores[s_ * ch + half:(s_ + 1) * ch, :] * vs
                o_a = ta if o_a is None else o_a + ta
            else:
                base = half * ch + (s_ - half) * half
                tb_ = scores[base:base + half, :] * vs
            o_b = tb_ if o_b is None else o_b + tb_
        oacc_ref[r0:r0 + half, :] += o_a
        oacc_ref[r0 + half:r0 + ch, :] += o_b

    o = oacc_ref[...]
    h2, l2, _ = _split3(o * o)
    ms = (_dot(h2, head_ones) + _dot(l2, head_ones)) * (1.0 / HEAD_DIM)
    o = o * lax.rsqrt(ms + RMS_EPS) * ng_ref[...]
    o_ref[...] = (o * _silu(z_g)).astype(BF)


def _hgrn(hg, lb_logits, norm_g, *, layer):
    b, s, _ = hg.shape
    depth = lb_logits.shape[0]
    gw = GROUP_WIDTH
    tb = min(HGRN_ROWS, s)
    return pl.pallas_call(
        functools.partial(_hgrn_kernel, layer=layer, depth=depth, tb=tb),
        grid=(b, s // tb),
        in_specs=[
            pl.BlockSpec((None, tb, 4 * gw), lambda bi, i: (bi, i, 0)),
            pl.BlockSpec((depth, gw), lambda bi, i: (0, 0)),
            pl.BlockSpec((1, gw), lambda bi, i: (0, 0)),
        ],
        out_specs=pl.BlockSpec((None, tb, gw), lambda bi, i: (bi, i, 0)),
        out_shape=jax.ShapeDtypeStruct((b, s, gw), BF),
        scratch_shapes=[pltpu.VMEM((2 * HEAD_DIM, gw), F32),
                        pltpu.VMEM((tb, gw), F32),
                        pltpu.VMEM((2, tb, gw), F32),
                        pltpu.VMEM((tb // HGRN_CHUNK, 2 * HEAD_DIM, gw), F32),
                        pltpu.VMEM((tb // HGRN_CHUNK, 2 * HEAD_DIM, gw), BF)],
        compiler_params=_params(2),
        name="hgrn2",
    )(hg, lb_logits, norm_g.reshape(1, gw))


def _attn_kernel(*refs, s_len, tq, has_bias, packed):
    if has_bias:
        q_ref, k_ref, v_ref, bk_ref, o_ref, s_ref = refs
    else:
        q_ref, k_ref, v_ref, o_ref, s_ref = refs
    pair = pl.program_id(1)
    n_q = s_len // tq
    n_g = tq // LANES
    causal = _iota2((tq, tq), 0) >= _iota2((tq, tq), 1)
    lane = _iota2((tq, LANES), 1)
    own_half = [(lane >> HEAD_SHIFT) == hh for hh in range(2)]
    keep = [m.astype(F32).astype(BF) for m in own_half]
    sum_lane = [HEAD_DIM, 0]
    ones_col = [(lane == sum_lane[hh]).astype(F32).astype(BF) for hh in range(2)]
    def head_lanes(hh):
        return slice(0, LANES) if packed else slice(hh * HEAD_SLOT, (hh + 1) * HEAD_SLOT)

    def pass1(qi, hh):
        lanes = head_lanes(hh)
        stage = s_ref.at[2 * (qi % 2) + hh]
        q = q_ref[qi * tq:(qi + 1) * tq, lanes]
        if packed:
            q = q * keep[hh]
        m_part = None
        for j in range(qi + 1):
            cols = slice(j * tq, (j + 1) * tq)
            sj = _dot(q, k_ref[lanes, cols])
            if has_bias:
                head = 2 * pair + hh
                row_id = _iota2((SUBLANES, tq), 0)
                sj = sj - jnp.sum(jnp.where(row_id == head, bk_ref[:, cols], 0.0), axis=0, keepdims=True)
            if j == qi:
                sj = jnp.where(causal, sj, NEG_BIG)
            stage[:, cols] = sj
            for g in range(n_g):
                blk = sj[:, g * LANES:(g + 1) * LANES]
                m_part = blk if m_part is None else jnp.maximum(m_part, blk)
        return jnp.broadcast_to(jnp.max(m_part, axis=1, keepdims=True), (tq, LANES))

    def pass2(qi, hh, m_b):
        lanes = head_lanes(hh)
        stage = s_ref.at[2 * (qi % 2) + hh]
        acc = jnp.zeros((tq, LANES), F32)
        for j in range(qi + 1):
            ps = [jnp.exp(stage[:, j * tq + g * LANES:j * tq + (g + 1) * LANES] - m_b).astype(BF)
                  for g in range(n_g)]
            vj = v_ref[j * tq:(j + 1) * tq, lanes]
            vj = (vj * keep[hh] if packed else vj) + ones_col[hh]
            acc = acc + _dot(jnp.concatenate(ps, axis=1), vj)
        row_sum = acc[:, sum_lane[hh]:sum_lane[hh] + 1]
        return jnp.where(own_half[hh], acc, 0.0) * (1.0 / row_sum)

    units = [(qi, hh) for qi in range(n_q) for hh in range(2)]
    ahead = 2
    maxes = {u: pass1(*u) for u in units[:ahead]}
    outs = {}
    for i, unit in enumerate(units):
        if i + ahead < len(units):
            maxes[units[i + ahead]] = pass1(*units[i + ahead])
        outs[unit] = pass2(*unit, maxes.pop(unit))
        qi, hh = unit
        if hh == 1:
            o_ref[qi * tq:(qi + 1) * tq, :] = (outs.pop((qi, 0)) + outs.pop((qi, 1))).astype(BF)


def _attention(q, k, v, bias_k=None, *, packed):
    b, s, _ = q.shape
    tq = min(ATT_Q_ROWS, s)
    has_bias = bias_k is not None
    pairw = LANES if packed else 2 * HEAD_SLOT
    seq = lambda n: pl.BlockSpec((None, s, n), lambda bi, p: (bi, 0, p))
    in_specs = [seq(pairw), pl.BlockSpec((None, pairw, s), lambda bi, p: (bi, p, 0)), seq(pairw)]
    args = [q, k, v]
    if has_bias:
        in_specs += [pl.BlockSpec((None, SUBLANES, s), lambda bi, p: (bi, 0, 0))]
        args += [bias_k]
    return pl.pallas_call(
        functools.partial(_attn_kernel, s_len=s, tq=tq, has_bias=has_bias, packed=packed),
        grid=(b, N_HEADS // 2),
        in_specs=in_specs,
        out_specs=seq(LANES),
        out_shape=jax.ShapeDtypeStruct((b, s, (N_HEADS // 2) * LANES), BF),
        scratch_shapes=[pltpu.VMEM((4, tq, s), F32)],
        compiler_params=_params(2),
        name="fox_attention" if has_bias else "mla_attention",
    )(*args)


def kernel(x, c, ada_w, ada_b, ln_g, ln_b, ffn1_w_in, ffn1_w_out, ffn2_w_in, ffn2_w_out, mix_w_in, mix_w_out, hgrn_lb_logits, hgrn_norm_g, mla_q_norm_g, mla_kv_norm_g, mla_w_uq, mla_w_ukv, fox_b_f, gmlp_ln_g, gmlp_ln_b, gmlp_w_s, gmlp_b_s):
    depth = ada_w.shape[0]
    b, s, d = x.shape
    assert d == N_HEADS * GROUP_WIDTH and s % GMLP_CHUNK == 0
    alpha = float((2 * depth) ** 0.25)
    mod_all = _modulation(c, ada_w, ada_b).reshape(depth, b, N_MOD, d)
    ct, st = _rope_tables(s)
    w1_in, w1_out = ffn1_w_in.astype(BF), ffn1_w_out.astype(BF)
    w2_in, w2_out = ffn2_w_in.astype(BF), ffn2_w_out.astype(BF)
    w_mix_out = mix_w_out.astype(BF)
    for l in range(depth):
        mod = mod_all[l]
        x = _ffn_sublayer(x, mod, w1_in, w1_out, ln_g, ln_b, l, mod_row=0, ln_row=0, alpha=alpha)
        w_big, w_uq2, w_ukv2 = _mix_in_weights(mix_w_in[l], mla_w_uq[l], mla_w_ukv[l])
        hg, mq, mk, mv, fq, fk, fv, fft, od = _mix_in(
            x, mod, w_big, w_uq2, w_ukv2, mla_q_norm_g[l], mla_kv_norm_g[l], ct, st, fox_b_f[l],
            gmlp_ln_g[l], gmlp_ln_b[l], gmlp_w_s[l], gmlp_b_s[l])
        oa = _hgrn(hg, hgrn_lb_logits, hgrn_norm_g[l], layer=l)
        ob = _attention(mq, mk, mv, packed=False)
        oc = _attention(fq, fk, fv, fft, packed=True)
        x = _ffn_sublayer(x, mod, w2_in, w2_out, ln_g, ln_b, l, mod_row=6, ln_row=2, alpha=alpha,
                          mix=(oa, ob, oc, od, w_mix_out))
    return x
```

```python
import functools

import jax
import jax.numpy as jnp
from jax import lax
from jax.experimental import pallas as pl
from jax.experimental.pallas import tpu as pltpu

BF = jnp.bfloat16
F32 = jnp.float32

N_HEADS = 4
GROUP_WIDTH = 256
HEAD_DIM = 64
HEAD_SHIFT = 6
HGRN_CHUNK = 16
CHUNK_SHIFT = 4
MLA_Q_LORA = 256
MLA_KV_LORA = 128
MLA_NOPE = 64
MLA_ROPE = 32
ROPE_THETA = 10000.0
GMLP_CHUNK = 128
N_MOD = 9
LN_EPS = 1e-5
RMS_EPS = 1e-6
LB_FLOOR = 1e-30
NEG_BIG = -1e30
LOG2_E = 1.4426950408889634

LANES = 128
SUBLANES = 8
HEAD_SLOT = 128
ROW_GROUP = 16
VMEM_LIMIT_BYTES = 56 * 1024 * 1024

FFN_ROWS = 1024
FFN_SUBTILES = 2
FFN_CHUNK = 256
MIX_ROWS = 1024
HGRN_ROWS = 256
ATT_Q_ROWS = 256


def _dot(a, b):
    return jnp.dot(a, b, preferred_element_type=F32)


def _dot_nt(a, b):
    return lax.dot_general(a, b, (((1,), (1,)), ((), ())), preferred_element_type=F32)


def _dot_tn(a, b):
    return lax.dot_general(a, b, (((0,), (0,)), ((), ())), preferred_element_type=F32)


def _sigmoid(x):
    return jax.nn.sigmoid(x)


def _silu(x):
    return x * _sigmoid(x)


def _log_sigmoid(x):
    return jnp.minimum(x, 0.0) - jnp.log1p(jnp.exp(-jnp.abs(x)))


def _gelu_tanh(x):
    c = 0.7978845608028654
    return 0.5 * x * (1.0 + jnp.tanh(c * (x + 0.044715 * (x * x * x))))


def _layer_norm(r, g, b):
    mu = jnp.mean(r, axis=-1, keepdims=True)
    d = r - mu
    var = jnp.mean(d * d, axis=-1, keepdims=True)
    return d * lax.rsqrt(var + LN_EPS) * g + b


def _rms_norm(r, g):
    return r * lax.rsqrt(jnp.mean(r * r, axis=-1, keepdims=True) + RMS_EPS) * g


def _split3(x):
    hi = x.astype(BF)
    r1 = x - hi.astype(F32)
    mid = r1.astype(BF)
    lo = (r1 - mid.astype(F32)).astype(BF)
    return hi, mid, lo


def _iota2(shape, dim):
    return lax.broadcasted_iota(jnp.int32, shape, dim)


def _resident(shape, index_map):
    return pl.BlockSpec(shape, index_map, pipeline_mode=pl.Buffered(1))


def _params(n_axes):
    return pltpu.CompilerParams(dimension_semantics=("arbitrary",) * n_axes,
                                vmem_limit_bytes=VMEM_LIMIT_BYTES)


def _mod_kernel(c_ref, w_ref, b_ref, o_ref):
    c = c_ref[...]
    o_ref[...] = _dot(_silu(c).astype(BF), w_ref[...].astype(BF)) + b_ref[...]


def _modulation(c, ada_w, ada_b):
    depth, d, n = ada_w.shape
    b = c.shape[0]
    tn = 1152 if n % 1152 == 0 else n
    return pl.pallas_call(
        _mod_kernel,
        grid=(depth, n // tn),
        in_specs=[
            pl.BlockSpec((b, d), lambda l, j: (0, 0)),
            pl.BlockSpec((None, d, tn), lambda l, j: (l, 0, j)),
            pl.BlockSpec((None, 1, tn), lambda l, j: (l, 0, j)),
        ],
        out_specs=pl.BlockSpec((None, b, tn), lambda l, j: (l, 0, j)),
        out_shape=jax.ShapeDtypeStruct((depth, b, n), F32),
        compiler_params=_params(2),
        name="adaln_modulation",
    )(c, ada_w, ada_b.reshape(depth, 1, n))


def _ffn_kernel(*refs, mod_row, ln_row, dff, ck, alpha, with_mix):
    if with_mix:
        (x_ref, mod_ref, oa_ref, ob_ref, oc_ref, od_ref, wm_ref, wi_ref, wo_ref, lng_ref, lnb_ref,
         o_ref, acc_ref, h_ref, y_ref, x1_ref) = refs
    else:
        x_ref, mod_ref, wi_ref, wo_ref, lng_ref, lnb_ref, o_ref, acc_ref, h_ref, y_ref = refs
    tm, d = x_ref.shape
    n_sub, ts = acc_ref.shape[0], acc_ref.shape[1]
    groups = [slice(r, r + ROW_GROUP) for r in range(0, ts, ROW_GROUP)]
    rep = lambda row: jnp.broadcast_to(row, (ROW_GROUP, d))
    sh = rep(mod_ref[mod_row:mod_row + 1, :])
    sc1 = rep(1.0 + mod_ref[mod_row + 1:mod_row + 2, :])
    gt_half = rep(0.5 * (1.0 + mod_ref[mod_row + 2:mod_row + 3, :]))
    lg, lb = rep(lng_ref[ln_row:ln_row + 1, :]), rep(lnb_ref[ln_row:ln_row + 1, :])
    xin_ref = x1_ref if with_mix else x_ref

    def in_tile(sub, rows):
        return slice(sub * ts + rows.start, sub * ts + rows.stop)

    def prologue(sub):
        if with_mix:
            gw = GROUP_WIDTH
            sl = slice(sub * ts, (sub + 1) * ts)
            y0 = (_dot(oa_ref[sl, :], wm_ref[0:gw, :]) + _dot(ob_ref[sl, :], wm_ref[gw:2 * gw, :])
                  + _dot(oc_ref[sl, :], wm_ref[2 * gw:3 * gw, :]) + _dot(od_ref[sl, :], wm_ref[3 * gw:4 * gw, :]))
            g2 = rep(1.0 + mod_ref[5:6, :])
            lg0, lb0 = rep(lng_ref[ln_row - 1:ln_row, :]), rep(lnb_ref[ln_row - 1:ln_row, :])

            def step(rows):
                x1 = _layer_norm(alpha * x_ref[in_tile(sub, rows), :] + g2 * y0[rows, :], lg0, lb0)
                x1_ref[in_tile(sub, rows), :] = x1
                h_ref[sub, rows, :] = (x1 * sc1 + sh).astype(BF)
        else:
            def step(rows):
                h_ref[sub, rows, :] = (x_ref[in_tile(sub, rows), :] * sc1 + sh).astype(BF)
        return [functools.partial(step, rows) for rows in groups]

    def up_chunks(sub):
        def step(c):
            gate = _dot(h_ref[sub], wi_ref[:, c * ck:(c + 1) * ck])
            up = _dot(h_ref[sub], wi_ref[:, dff + c * ck:dff + (c + 1) * ck])
            acc_ref[sub, :, c * ck:(c + 1) * ck] = (_silu(gate) * up).astype(BF)
        return [functools.partial(step, c) for c in range(dff // ck)]

    def epilogue(sub):
        def step(rows):
            o_ref[in_tile(sub, rows), :] = _layer_norm(
                alpha * xin_ref[in_tile(sub, rows), :] + gt_half * y_ref[sub, rows, :], lg, lb)
        return [functools.partial(step, rows) for rows in groups]

    def run(main, filler=()):
        filler = list(filler)
        per = -(-len(filler) // max(len(main), 1))
        for step in main:
            step()
            for _ in range(per):
                if filler:
                    filler.pop(0)()
        for step in filler:
            step()

    run(prologue(0))
    pending = []
    for sub in range(n_sub):
        nxt = prologue(sub + 1) if sub + 1 < n_sub else []
        run(up_chunks(sub), pending + nxt)
        y_ref[sub] = _dot(acc_ref[sub], wo_ref[...])
        pending = epilogue(sub)
    run(pending)


def _ffn_sublayer(x, mod, w_in, w_out, ln_g, ln_b, layer, *, mod_row, ln_row, alpha, mix=None):
    b, s, d = x.shape
    dff = w_out.shape[1]
    ck = FFN_CHUNK if dff % FFN_CHUNK == 0 else dff
    tm = min(FFN_ROWS, s)
    n_sub = FFN_SUBTILES if tm % (FFN_SUBTILES * ROW_GROUP) == 0 else 1
    tok = lambda n: pl.BlockSpec((None, tm, n), lambda bi, i: (bi, i, 0))
    lay = lambda bi, i: (layer, 0, 0)
    in_specs = [tok(d), pl.BlockSpec((None, N_MOD, d), lambda bi, i: (bi, 0, 0))]
    args = [x, mod]
    if mix is not None:
        oa, ob, oc, od, w_mix = mix
        gw = GROUP_WIDTH
        in_specs += [tok(gw), tok(gw), tok(gw), tok(gw), _resident((None, d, d), lay)]
        args += [oa, ob, oc, od, w_mix]
    in_specs += [_resident((None, d, 2 * dff), lay), _resident((None, dff, d), lay),
                 _resident((None,) + ln_g.shape[1:], lay), _resident((None,) + ln_b.shape[1:], lay)]
    args += [w_in, w_out, ln_g, ln_b]
    return pl.pallas_call(
        functools.partial(_ffn_kernel, mod_row=mod_row, ln_row=ln_row, dff=dff, ck=ck, alpha=alpha,
                          with_mix=mix is not None),
        grid=(b, s // tm),
        in_specs=in_specs,
        out_specs=tok(d),
        out_shape=jax.ShapeDtypeStruct((b, s, d), F32),
        scratch_shapes=[pltpu.VMEM((n_sub, tm // n_sub, dff), BF), pltpu.VMEM((n_sub, tm // n_sub, d), BF),
                        pltpu.VMEM((n_sub, tm // n_sub, d), F32)]
        + ([pltpu.VMEM((tm, d), F32)] if mix is not None else []),
        compiler_params=_params(2),
        name="mix_out_ffn" if mix is not None else "ffn_sublayer",
    )(*args)


_C_CQ = 0
_C_CKV = 256
_C_KRF = 384
_C_DU = 512
_C_DV = 768
_C_FOX = 1024
_C_HG = 1792
_C_END = 2816
CUM_ROWS = 256


def _mix_in_kernel(x_ref, mod_ref, w_ref, wuq_ref, wukv_ref, qg_ref, kvg_ref, ct_ref, st_ref,
                   fb_ref, dlg_ref, dlb_ref, ws_ref, bs_ref,
                   hg_ref, mq_ref, mk_ref, mv_ref, fq_ref, fk_ref, fv_ref, fft_ref, od_ref,
                   carry_ref, h_ref, *, tm, mla_scale, fox_scale):
    d = x_ref.shape[1]
    sh = jnp.broadcast_to(mod_ref[3:4, :], (ROW_GROUP, d))
    sc1 = jnp.broadcast_to(1.0 + mod_ref[4:5, :], (ROW_GROUP, d))
    for r in range(0, tm, ROW_GROUP):
        h_ref[r:r + ROW_GROUP, :] = (x_ref[r:r + ROW_GROUP, :] * sc1 + sh).astype(BF)
    h = h_ref[...]
    gw = GROUP_WIDTH

    @pl.when(pl.program_id(1) == 0)
    def _():
        carry_ref[...] = jnp.zeros_like(carry_ref)

    s1 = _dot(h, w_ref[:, _C_CQ:_C_FOX])
    c_q = s1[:, _C_CQ:_C_CKV]
    c_kv = s1[:, _C_CKV:_C_KRF]
    krf = s1[:, _C_KRF:_C_DU]
    d_u = s1[:, _C_DU:_C_DV]
    d_v = s1[:, _C_DV:_C_FOX]

    fox = _dot(h, w_ref[:, _C_FOX:_C_HG])
    fq_ref[...] = (fox[:, 0:gw] * fox_scale).astype(BF)
    fk_ref[...] = fox[:, gw:2 * gw].T.astype(BF)
    fv_ref[...] = fox[:, 2 * gw:3 * gw].astype(BF)
    nq = _rms_norm(c_q, qg_ref[...]).astype(BF)
    nkv = _rms_norm(c_kv, kvg_ref[...]).astype(BF)
    logf = _log_sigmoid(krf + fb_ref[...])
    u = _gelu_tanh(d_u)
    vn = _layer_norm(_gelu_tanh(d_v), dlg_ref[...], dlb_ref[...])

    q2 = _dot(nq, wuq_ref[...])
    kv2 = _dot(nkv, wukv_ref[...])
    cr = min(CUM_ROWS, tm)
    tri = (_iota2((cr, cr), 0) >= _iota2((cr, cr), 1)).astype(BF)
    pick = (_iota2((SUBLANES, LANES), 0) == _iota2((SUBLANES, LANES), 1)).astype(BF)
    run = carry_ref[...]
    for j in range(tm // cr):
        f_hi, f_mid, f_lo = _split3(logf[j * cr:(j + 1) * cr, :])
        cum = _dot(tri, f_hi) + _dot(tri, f_mid) + _dot(tri, f_lo) + run
        run = cum[cr - 1:cr, :]
        c_hi, c_mid, c_lo = _split3(cum)
        fft_ref[:, j * cr:(j + 1) * cr] = _dot_nt(pick, c_hi) + _dot_nt(pick, c_mid) + _dot_nt(pick, c_lo)
    carry_ref[...] = run
    cc = GMLP_CHUNK
    causal = _iota2((cc, cc), 0) >= _iota2((cc, cc), 1)
    lane_group = _iota2((cc, gw), 1) >> HEAD_SHIFT
    w_cat = jnp.concatenate([jnp.where(causal, ws_ref[g], 0.0) for g in range(N_HEADS)],
                            axis=1).astype(BF)
    for j in range(tm // cc):
        vj = vn[j * cc:(j + 1) * cc, :]
        v_stack = jnp.concatenate([jnp.where(lane_group == g, vj, 0.0) for g in range(N_HEADS)],
                                  axis=0).astype(BF)
        mixed = bs_ref[...] + _dot(w_cat, v_stack)
        od_ref[j * cc:(j + 1) * cc, :] = (u[j * cc:(j + 1) * cc, :] * mixed).astype(BF)

    hg_ref[...] = _dot(h, w_ref[:, _C_HG:_C_END])
    ct = ct_ref[...]
    st = st_ref[...]
    ct4 = jnp.concatenate([ct] * N_HEADS, axis=1)
    st4 = jnp.concatenate([st] * N_HEADS, axis=1)
    nslot = N_HEADS * HEAD_SLOT
    q_rot = q2[:, :nslot] * ct4 + q2[:, nslot:] * st4
    mq_ref[...] = (q_rot * mla_scale).astype(BF)
    half = MLA_ROPE // 2
    lane = _iota2((tm, HEAD_SLOT), 1)
    kr = jnp.where(lane >= MLA_NOPE, krf, 0.0)
    kr_swap = jnp.where(lane < MLA_NOPE + half, pltpu.roll(kr, HEAD_SLOT - half, 1), pltpu.roll(kr, half, 1))
    kr_rot = kr * ct + kr_swap * st
    mk_ref[...] = (kv2[:, :nslot] + jnp.concatenate([kr_rot] * N_HEADS, axis=1)).T.astype(BF)
    mv_ref[...] = kv2[:, nslot:].astype(BF)


def _place_heads(w, width, offsets):
    cols = []
    for hd in range(N_HEADS):
        blk = w[:, hd * width:(hd + 1) * width]
        off = offsets[hd]
        cols.append(jnp.pad(blk, ((0, 0), (off, HEAD_SLOT - off - width))))
    return jnp.concatenate(cols, axis=1)


def _mix_in_weights(mix_w_in, mla_w_uq, mla_w_ukv):
    d = mix_w_in.shape[0]
    gw = GROUP_WIDTH
    w = mix_w_in.astype(BF)
    n_hg = 4 * gw
    n_a = n_hg + MLA_Q_LORA + MLA_KV_LORA
    b_kr = w[:, n_a:n_a + MLA_ROPE]
    o = n_a + MLA_ROPE
    c_qkv = w[:, o:o + 3 * gw]
    c_f = w[:, o + 3 * gw:o + 3 * gw + N_HEADS]
    d_uv = w[:, o + 3 * gw + N_HEADS:]
    z = lambda n: jnp.zeros((d, n), BF)
    w_big = jnp.concatenate([
        w[:, n_hg:n_a], c_f, z(MLA_NOPE - N_HEADS), b_kr, z(HEAD_SLOT - MLA_NOPE - MLA_ROPE),
        d_uv, c_qkv, w[:, :n_hg]], axis=1)
    assert w_big.shape[1] == _C_END
    half = MLA_ROPE // 2
    qk = MLA_NOPE + MLA_ROPE
    uq = mla_w_uq.reshape(MLA_Q_LORA, N_HEADS, qk)
    zq = lambda n: jnp.zeros((MLA_Q_LORA, N_HEADS, n), mla_w_uq.dtype)
    uq_n = jnp.concatenate([uq, zq(HEAD_SLOT - qk)], axis=2)
    uq_s = jnp.concatenate([zq(MLA_NOPE), uq[:, :, MLA_NOPE + half:], uq[:, :, MLA_NOPE:MLA_NOPE + half],
                            zq(HEAD_SLOT - qk)], axis=2)
    w_uq2 = jnp.concatenate([uq_n.reshape(MLA_Q_LORA, -1), uq_s.reshape(MLA_Q_LORA, -1)], axis=1).astype(BF)
    ukv = mla_w_ukv.reshape(MLA_KV_LORA, N_HEADS, MLA_NOPE + HEAD_DIM)
    zero_off = (0,) * N_HEADS
    pair_off = tuple((hd % 2) * HEAD_DIM for hd in range(N_HEADS))
    k_n = _place_heads(ukv[:, :, :MLA_NOPE].reshape(MLA_KV_LORA, -1), MLA_NOPE, zero_off)
    v_p = _place_heads(ukv[:, :, MLA_NOPE:].reshape(MLA_KV_LORA, -1), HEAD_DIM, pair_off)
    w_ukv2 = jnp.concatenate([k_n, v_p], axis=1).astype(BF)
    return w_big, w_uq2, w_ukv2


def _rope_tables(s):
    half = MLA_ROPE // 2
    inv_freq = ROPE_THETA ** (-jnp.arange(half, dtype=F32) / half)
    ang = jnp.arange(s, dtype=F32)[:, None] * inv_freq[None, :]
    cos, sin = jnp.cos(ang), jnp.sin(ang)
    tail = HEAD_SLOT - MLA_NOPE - MLA_ROPE
    ct = jnp.concatenate([jnp.ones((s, MLA_NOPE), F32), cos, cos, jnp.zeros((s, tail), F32)], axis=1)
    st = jnp.concatenate([jnp.zeros((s, MLA_NOPE), F32), -sin, sin, jnp.zeros((s, tail), F32)], axis=1)
    return ct, st


def _mix_in(x, mod, w_big, w_uq2, w_ukv2, q_norm_g, kv_norm_g, ct, st, fox_b_f,
            gmlp_ln_g, gmlp_ln_b, gmlp_w_s, gmlp_b_s):
    b, s, d = x.shape
    tm = min(MIX_ROWS, s)
    nslot = N_HEADS * HEAD_SLOT
    gw = GROUP_WIDTH
    fb = jnp.pad(fox_b_f.reshape(1, N_HEADS), ((0, 0), (0, LANES - N_HEADS)))
    bias_tile = jnp.repeat(gmlp_b_s.T, HEAD_DIM, axis=1)
    full2 = lambda bi, i: (0, 0)
    tok = lambda n: pl.BlockSpec((None, tm, n), lambda bi, i: (bi, i, 0))
    shp = lambda n, dt: jax.ShapeDtypeStruct((b, s, n), dt)
    tok_t = lambda n: pl.BlockSpec((None, n, tm), lambda bi, i: (bi, 0, i))
    shp_t = lambda n: jax.ShapeDtypeStruct((b, n, s), BF)
    return pl.pallas_call(
        functools.partial(_mix_in_kernel, tm=tm, mla_scale=float((MLA_NOPE + MLA_ROPE) ** -0.5 * LOG2_E),
                          fox_scale=float(HEAD_DIM ** -0.5 * LOG2_E)),
        grid=(b, s // tm),
        in_specs=[
            tok(d),
            pl.BlockSpec((None, N_MOD, d), lambda bi, i: (bi, 0, 0)),
            _resident(w_big.shape, full2),
            _resident(w_uq2.shape, full2),
            _resident(w_ukv2.shape, full2),
            _resident((1, MLA_Q_LORA), full2),
            _resident((1, MLA_KV_LORA), full2),
            pl.BlockSpec((tm, HEAD_SLOT), lambda bi, i: (i, 0)),
            pl.BlockSpec((tm, HEAD_SLOT), lambda bi, i: (i, 0)),
            _resident((1, LANES), full2),
            _resident((1, gw), full2),
            _resident((1, gw), full2),
            _resident(gmlp_w_s.shape, lambda bi, i: (0, 0, 0)),
            _resident(bias_tile.shape, full2),
        ],
        out_specs=[tok(4 * gw), tok(nslot), tok_t(nslot), tok(nslot), tok(gw), tok_t(gw), tok(gw),
                   pl.BlockSpec((None, SUBLANES, tm), lambda bi, i: (bi, 0, i)), tok(gw)],
        out_shape=[shp(4 * gw, F32), shp(nslot, BF), shp_t(nslot), shp(nslot, BF),
                   shp(gw, BF), shp_t(gw), shp(gw, BF),
                   jax.ShapeDtypeStruct((b, SUBLANES, s), F32), shp(gw, BF)],
        scratch_shapes=[pltpu.VMEM((1, LANES), F32), pltpu.VMEM((tm, d), BF)],
        compiler_params=_params(2),
        name="mixer_in",
    )(x, mod, w_big, w_uq2, w_ukv2, q_norm_g.reshape(1, -1), kv_norm_g.reshape(1, -1), ct, st, fb,
      gmlp_ln_g.reshape(1, gw), gmlp_ln_b.reshape(1, gw), gmlp_w_s, bias_tile)


def _hgrn_kernel(hg_ref, lbl_ref, ng_ref, o_ref, st_ref, oacc_ref, gk_ref, kv_ref, stb_ref,
                 *, layer, depth, tb):
    gw = GROUP_WIDTH
    ch = HGRN_CHUNK
    half = ch // 2
    n_ch = tb // ch

    @pl.when(pl.program_id(1) == 0)
    def _():
        st_ref[...] = jnp.zeros_like(st_ref)

    rows = [lbl_ref[j:j + 1, :] for j in range(depth)]
    top = functools.reduce(jnp.maximum, rows)
    ex = [jnp.exp(r - top) for r in rows]
    sm = [e / sum(ex) for e in ex]
    lb = sum(sm[:layer + 1]) - sm[0]
    lb_floor = jnp.maximum(lb, LB_FLOOR)

    z_q = hg_ref[:, 0:gw]
    z_f = hg_ref[:, gw:2 * gw]
    val = hg_ref[:, 2 * gw:3 * gw]
    z_g = hg_ref[:, 3 * gw:4 * gw]
    log_f = jnp.log(lb_floor + (1.0 - lb) * _sigmoid(z_f))
    kk = (1.0 - lb) * _sigmoid(-z_f) - (lb_floor - lb)
    qf = _silu(z_q)

    r_i = _iota2((tb, tb), 0)
    c_i = _iota2((tb, tb), 1)
    same_chunk = (r_i >> CHUNK_SHIFT) == (c_i >> CHUNK_SHIFT)
    tri_blk = (same_chunk & (r_i >= c_i)).astype(BF)
    ones_blk = same_chunk.astype(BF)
    hi, mid, lo = _split3(log_f)
    g_cum = _dot(tri_blk, hi) + _dot(tri_blk, mid) + _dot(tri_blk, lo)
    g_last = _dot(ones_blk, hi) + _dot(ones_blk, mid) + _dot(ones_blk, lo)
    q_dec = qf * jnp.exp(g_cum)
    k_end = kk * jnp.exp(g_last - g_cum)
    dec = jnp.exp(g_last)

    head_ones = ((_iota2((gw, gw), 0) >> HEAD_SHIFT) == (_iota2((gw, gw), 1) >> HEAD_SHIFT)).astype(BF)
    lane_head = _iota2((ch, gw), 1) >> HEAD_SHIFT
    head_lanes = [(lane_head == hd).astype(F32) for hd in range(N_HEADS)]
    pair_lanes = [(lane_head >> 1 == j).astype(F32) for j in range(2)]
    low_half = (_iota2((ch, LANES), 1) < HEAD_DIM).astype(F32)
    high_half = 1.0 - low_half

    g2 = g_cum * LOG2_E
    gk_ref[0] = g2
    gk_ref[1] = kk
    t_half = _iota2((half, gw), 0)
    causal_half = [t_half >= s_ for s_ in range(half)]

    def row(ref_row):
        return jnp.broadcast_to(ref_row, (half, gw))

    for c in range(n_ch):
        r0 = c * ch
        v_c = val[r0:r0 + ch, :]
        v_lo, v_hi = v_c[:, :LANES], v_c[:, LANES:]
        v_stack = jnp.concatenate([v_lo * low_half, v_lo * high_half, v_hi * low_half, v_hi * high_half],
                                  axis=0).astype(BF)
        ke_c = k_end[r0:r0 + ch, :]
        k_stack = jnp.concatenate([ke_c * head_lanes[hd] for hd in range(N_HEADS)],
                                  axis=0).astype(BF)
        kv_ref[c] = _dot_tn(v_stack, k_stack)
    for c in range(n_ch):
        st_c = st_ref[...]
        stb_ref[c] = st_c.astype(BF)
        st_ref[...] = st_c * dec[c * ch:c * ch + 1, :] + kv_ref[c]
    for c in range(n_ch):
        r0 = c * ch
        qd_c = q_dec[r0:r0 + ch, :]
        q_pairs = jnp.concatenate([qd_c * pair_lanes[0], qd_c * pair_lanes[1]], axis=0).astype(BF)
        inter = _dot_nt(q_pairs, stb_ref[c])
        oacc_ref[r0:r0 + ch, :] = jnp.concatenate([inter[:ch, :], inter[ch:, :]], axis=1)
    for c in range(n_ch):
        r0 = c * ch
        g_a, g_b = g2[r0:r0 + half, :], g2[r0 + half:r0 + ch, :]
        q_a, q_b = qf[r0:r0 + half, :], qf[r0 + half:r0 + ch, :]
        early, late = [], []
        for s_ in range(ch):
            gs = row(gk_ref[0, r0 + s_:r0 + s_ + 1, :])
            ks = row(gk_ref[1, r0 + s_:r0 + s_ + 1, :])
            if s_ < half:
                pa = jnp.where(causal_half[s_], jnp.exp2(g_a - gs), 0.0) * (q_a * ks)
                pb = jnp.exp2(g_b - gs) * (q_b * ks)
                early += [pa, pb]
            else:
                late.append(jnp.where(causal_half[s_ - half], jnp.exp2(g_b - gs), 0.0) * (q_b * ks))
        p_rows = jnp.concatenate(early + late, axis=0).astype(BF)
        scores = _dot(p_rows, head_ones)
        o_a = o_b = None
        for s_ in range(ch):
            vs = row(hg_ref[r0 + s_:r0 + s_ + 1, 2 * gw:3 * gw])
            if s_ < half:
                ta = scores[s_ * ch:s_ * ch + half, :] * vs
                tb_ = scores[s_ * ch + half:(s_ + 1) * ch, :] * vs
                o_a = ta if o_a is None else o_a + ta
            else:
                base = half * ch + (s_ - half) * half
                tb_ = scores[base:base + half, :] * vs
            o_b = tb_ if o_b is None else o_b + tb_
        oacc_ref[r0:r0 + half, :] += o_a
        oacc_ref[r0 + half:r0 + ch, :] += o_b

    o = oacc_ref[...]
    h2, l2, _ = _split3(o * o)
    ms = (_dot(h2, head_ones) + _dot(l2, head_ones)) * (1.0 / HEAD_DIM)
    o = o * lax.rsqrt(ms + RMS_EPS) * ng_ref[...]
    o_ref[...] = (o * _silu(z_g)).astype(BF)


def _hgrn(hg, lb_logits, norm_g, *, layer):
    b, s, _ = hg.shape
    depth = lb_logits.shape[0]
    gw = GROUP_WIDTH
    tb = min(HGRN_ROWS, s)
    return pl.pallas_call(
        functools.partial(_hgrn_kernel, layer=layer, depth=depth, tb=tb),
        grid=(b, s // tb),
        in_specs=[
            pl.BlockSpec((None, tb, 4 * gw), lambda bi, i: (bi, i, 0)),
            pl.BlockSpec((depth, gw), lambda bi, i: (0, 0)),
            pl.BlockSpec((1, gw), lambda bi, i: (0, 0)),
        ],
        out_specs=pl.BlockSpec((None, tb, gw), lambda bi, i: (bi, i, 0)),
        out_shape=jax.ShapeDtypeStruct((b, s, gw), BF),
        scratch_shapes=[pltpu.VMEM((2 * HEAD_DIM, gw), F32),
                        pltpu.VMEM((tb, gw), F32),
                        pltpu.VMEM((2, tb, gw), F32),
                        pltpu.VMEM((tb // HGRN_CHUNK, 2 * HEAD_DIM, gw), F32),
                        pltpu.VMEM((tb // HGRN_CHUNK, 2 * HEAD_DIM, gw), BF)],
        compiler_params=_params(2),
        name="hgrn2",
    )(hg, lb_logits, norm_g.reshape(1, gw))


def _attn_kernel(*refs, s_len, tq, has_bias, packed):
    if has_bias:
        q_ref, k_ref, v_ref, bk_ref, o_ref, s_ref = refs
    else:
        q_ref, k_ref, v_ref, o_ref, s_ref = refs
    pair = pl.program_id(1)
    n_q = s_len // tq
    n_g = tq // LANES
    causal = _iota2((tq, tq), 0) >= _iota2((tq, tq), 1)
    lane = _iota2((tq, LANES), 1)
    own_half = [(lane >> HEAD_SHIFT) == hh for hh in range(2)]
    keep = [m.astype(F32).astype(BF) for m in own_half]
    sum_lane = [HEAD_DIM, 0]
    ones_col = [(lane == sum_lane[hh]).astype(F32).astype(BF) for hh in range(2)]
    def head_lanes(hh):
        return slice(0, LANES) if packed else slice(hh * HEAD_SLOT, (hh + 1) * HEAD_SLOT)

    def pass1(qi, hh):
        lanes = head_lanes(hh)
        stage = s_ref.at[2 * (qi % 2) + hh]
        q = q_ref[qi * tq:(qi + 1) * tq, lanes]
        if packed:
            q = q * keep[hh]
        m_part = None
        for j in range(qi + 1):
            cols = slice(j * tq, (j + 1) * tq)
            sj = _dot(q, k_ref[lanes, cols])
            if has_bias:
                head = 2 * pair + hh
                row_id = _iota2((SUBLANES, tq), 0)
                f_s = jnp.sum(jnp.where(row_id == head, bk_ref[:, cols], 0.0), axis=0, keepdims=True)
                sj = sj - f_s * LOG2_E
            if j == qi:
                sj = jnp.where(causal, sj, NEG_BIG)
            stage[:, cols] = sj
            for g in range(n_g):
                blk = sj[:, g * LANES:(g + 1) * LANES]
                m_part = blk if m_part is None else jnp.maximum(m_part, blk)
        return jnp.broadcast_to(jnp.max(m_part, axis=1, keepdims=True), (tq, LANES))

    def pass2(qi, hh, m_b):
        lanes = head_lanes(hh)
        stage = s_ref.at[2 * (qi % 2) + hh]
        acc = jnp.zeros((tq, LANES), F32)
        for j in range(qi + 1):
            ps = [jnp.exp2(stage[:, j * tq + g * LANES:j * tq + (g + 1) * LANES] - m_b).astype(BF)
                  for g in range(n_g)]
            vj = v_ref[j * tq:(j + 1) * tq, lanes]
            vj = (vj * keep[hh] if packed else vj) + ones_col[hh]
            acc = acc + _dot(jnp.concatenate(ps, axis=1), vj)
        row_sum = acc[:, sum_lane[hh]:sum_lane[hh] + 1]
        return jnp.where(own_half[hh], acc, 0.0) * (1.0 / row_sum)

    units = [(qi, hh) for qi in range(n_q) for hh in range(2)]
    ahead = 3
    maxes = {u: pass1(*u) for u in units[:ahead]}
    outs = {}
    for i, unit in enumerate(units):
        if i + ahead < len(units):
            maxes[units[i + ahead]] = pass1(*units[i + ahead])
        outs[unit] = pass2(*unit, maxes.pop(unit))
        qi, hh = unit
        if hh == 1:
            o_ref[qi * tq:(qi + 1) * tq, :] = (outs.pop((qi, 0)) + outs.pop((qi, 1))).astype(BF)


def _attention(q, k, v, bias_k=None, *, packed):
    b, s, _ = q.shape
    tq = min(ATT_Q_ROWS, s)
    has_bias = bias_k is not None
    pairw = LANES if packed else 2 * HEAD_SLOT
    seq = lambda n: pl.BlockSpec((None, s, n), lambda bi, p: (bi, 0, p))
    in_specs = [seq(pairw), pl.BlockSpec((None, pairw, s), lambda bi, p: (bi, p, 0)), seq(pairw)]
    args = [q, k, v]
    if has_bias:
        in_specs += [pl.BlockSpec((None, SUBLANES, s), lambda bi, p: (bi, 0, 0))]
        args += [bias_k]
    return pl.pallas_call(
        functools.partial(_attn_kernel, s_len=s, tq=tq, has_bias=has_bias, packed=packed),
        grid=(b, N_HEADS // 2),
        in_specs=in_specs,
        out_specs=seq(LANES),
        out_shape=jax.ShapeDtypeStruct((b, s, (N_HEADS // 2) * LANES), BF),
        scratch_shapes=[pltpu.VMEM((4, tq, s), F32)],
        compiler_params=_params(2),
        name="fox_attention" if has_bias else "mla_attention",
    )(*args)


def kernel(x, c, ada_w, ada_b, ln_g, ln_b, ffn1_w_in, ffn1_w_out, ffn2_w_in, ffn2_w_out, mix_w_in, mix_w_out, hgrn_lb_logits, hgrn_norm_g, mla_q_norm_g, mla_kv_norm_g, mla_w_uq, mla_w_ukv, fox_b_f, gmlp_ln_g, gmlp_ln_b, gmlp_w_s, gmlp_b_s):
    depth = ada_w.shape[0]
    b, s, d = x.shape
    assert d == N_HEADS * GROUP_WIDTH and s % GMLP_CHUNK == 0
    alpha = float((2 * depth) ** 0.25)
    mod_all = _modulation(c, ada_w, ada_b).reshape(depth, b, N_MOD, d)
    ct, st = _rope_tables(s)
    w1_in, w1_out = ffn1_w_in.astype(BF), ffn1_w_out.astype(BF)
    w2_in, w2_out = ffn2_w_in.astype(BF), ffn2_w_out.astype(BF)
    w_mix_out = mix_w_out.astype(BF)
    for l in range(depth):
        mod = mod_all[l]
        x = _ffn_sublayer(x, mod, w1_in, w1_out, ln_g, ln_b, l, mod_row=0, ln_row=0, alpha=alpha)
        w_big, w_uq2, w_ukv2 = _mix_in_weights(mix_w_in[l], mla_w_uq[l], mla_w_ukv[l])
        hg, mq, mk, mv, fq, fk, fv, fft, od = _mix_in(
            x, mod, w_big, w_uq2, w_ukv2, mla_q_norm_g[l], mla_kv_norm_g[l], ct, st, fox_b_f[l],
            gmlp_ln_g[l], gmlp_ln_b[l], gmlp_w_s[l], gmlp_b_s[l])
        oa = _hgrn(hg, hgrn_lb_logits, hgrn_norm_g[l], layer=l)
        ob = _attention(mq, mk, mv, packed=False)
        oc = _attention(fq, fk, fv, fft, packed=True)
        x = _ffn_sublayer(x, mod, w2_in, w2_out, ln_g, ln_b, l, mod_row=6, ln_row=2, alpha=alpha,
                          mix=(oa, ob, oc, od, w_mix_out))
    return x
```

```python
import functools

import jax
import jax.numpy as jnp
from jax import lax
from jax.experimental import pallas as pl
from jax.experimental.pallas import tpu as pltpu

BF = jnp.bfloat16
F32 = jnp.float32

N_HEADS = 4
GROUP_WIDTH = 256
HEAD_DIM = 64
HEAD_SHIFT = 6
HGRN_CHUNK = 16
CHUNK_SHIFT = 4
MLA_Q_LORA = 256
MLA_KV_LORA = 128
MLA_NOPE = 64
MLA_ROPE = 32
ROPE_THETA = 10000.0
GMLP_CHUNK = 128
N_MOD = 9
LN_EPS = 1e-5
RMS_EPS = 1e-6
LB_FLOOR = 1e-30
NEG_BIG = -1e30
LOG2_E = 1.4426950408889634

LANES = 128
SUBLANES = 8
HEAD_SLOT = 128
ROW_GROUP = 16
VMEM_LIMIT_BYTES = 56 * 1024 * 1024

FFN_ROWS = 1024
FFN_SUBTILES = 2
FFN_CHUNK = 256
MIX_ROWS = 1024
HGRN_ROWS = 256
ATT_Q_ROWS = 256


def _dot(a, b):
    return jnp.dot(a, b, preferred_element_type=F32)


def _dot_nt(a, b):
    return lax.dot_general(a, b, (((1,), (1,)), ((), ())), preferred_element_type=F32)


def _dot_tn(a, b):
    return lax.dot_general(a, b, (((0,), (0,)), ((), ())), preferred_element_type=F32)


def _sigmoid(x):
    return jax.nn.sigmoid(x)


def _silu(x):
    return x * _sigmoid(x)


def _log_sigmoid(x):
    return jnp.minimum(x, 0.0) - jnp.log1p(jnp.exp(-jnp.abs(x)))


def _gelu_tanh(x):
    c = 0.7978845608028654
    return 0.5 * x * (1.0 + jnp.tanh(c * (x + 0.044715 * (x * x * x))))


def _layer_norm(r, g, b):
    mu = jnp.mean(r, axis=-1, keepdims=True)
    d = r - mu
    var = jnp.mean(d * d, axis=-1, keepdims=True)
    return d * lax.rsqrt(var + LN_EPS) * g + b


def _rms_norm(r, g):
    return r * lax.rsqrt(jnp.mean(r * r, axis=-1, keepdims=True) + RMS_EPS) * g


def _split3(x):
    hi = x.astype(BF)
    r1 = x - hi.astype(F32)
    mid = r1.astype(BF)
    lo = (r1 - mid.astype(F32)).astype(BF)
    return hi, mid, lo


def _iota2(shape, dim):
    return lax.broadcasted_iota(jnp.int32, shape, dim)


def _resident(shape, index_map):
    return pl.BlockSpec(shape, index_map, pipeline_mode=pl.Buffered(1))


def _params(n_axes):
    return pltpu.CompilerParams(dimension_semantics=("arbitrary",) * n_axes,
                                vmem_limit_bytes=VMEM_LIMIT_BYTES)


def _mod_kernel(c_ref, w_ref, b_ref, o_ref):
    c = c_ref[...]
    o_ref[...] = _dot(_silu(c).astype(BF), w_ref[...].astype(BF)) + b_ref[...]


def _modulation(c, ada_w, ada_b):
    depth, d, n = ada_w.shape
    b = c.shape[0]
    tn = 1152 if n % 1152 == 0 else n
    return pl.pallas_call(
        _mod_kernel,
        grid=(depth, n // tn),
        in_specs=[
            pl.BlockSpec((b, d), lambda l, j: (0, 0)),
            pl.BlockSpec((None, d, tn), lambda l, j: (l, 0, j)),
            pl.BlockSpec((None, 1, tn), lambda l, j: (l, 0, j)),
        ],
        out_specs=pl.BlockSpec((None, b, tn), lambda l, j: (l, 0, j)),
        out_shape=jax.ShapeDtypeStruct((depth, b, n), F32),
        compiler_params=_params(2),
        name="adaln_modulation",
    )(c, ada_w, ada_b.reshape(depth, 1, n))


def _ffn_kernel(*refs, mod_row, ln_row, dff, ck, alpha, with_mix):
    if with_mix:
        (x_ref, mod_ref, oa_ref, ob_ref, oc_ref, od_ref, wm_ref, wi_ref, wo_ref, lng_ref, lnb_ref,
         o_ref, acc_ref, h_ref, y_ref, x1_ref) = refs
    else:
        x_ref, mod_ref, wi_ref, wo_ref, lng_ref, lnb_ref, o_ref, acc_ref, h_ref, y_ref = refs
    tm, d = x_ref.shape
    n_sub, ts = acc_ref.shape[0], acc_ref.shape[1]
    groups = [slice(r, r + ROW_GROUP) for r in range(0, ts, ROW_GROUP)]
    rep = lambda row: jnp.broadcast_to(row, (ROW_GROUP, d))
    sh = rep(mod_ref[mod_row:mod_row + 1, :])
    sc1 = rep(1.0 + mod_ref[mod_row + 1:mod_row + 2, :])
    gt_half = rep(0.5 * (1.0 + mod_ref[mod_row + 2:mod_row + 3, :]))
    lg, lb = rep(lng_ref[ln_row:ln_row + 1, :]), rep(lnb_ref[ln_row:ln_row + 1, :])
    xin_ref = x1_ref if with_mix else x_ref

    def in_tile(sub, rows):
        return slice(sub * ts + rows.start, sub * ts + rows.stop)

    def prologue(sub):
        if with_mix:
            gw = GROUP_WIDTH
            sl = slice(sub * ts, (sub + 1) * ts)
            y0 = (_dot(oa_ref[sl, :], wm_ref[0:gw, :]) + _dot(ob_ref[sl, :], wm_ref[gw:2 * gw, :])
                  + _dot(oc_ref[sl, :], wm_ref[2 * gw:3 * gw, :]) + _dot(od_ref[sl, :], wm_ref[3 * gw:4 * gw, :]))
            g2 = rep(1.0 + mod_ref[5:6, :])
            lg0, lb0 = rep(lng_ref[ln_row - 1:ln_row, :]), rep(lnb_ref[ln_row - 1:ln_row, :])

            def step(rows, gate_zero=None):
                xs = x_ref[in_tile(sub, rows), :]
                xs = xs if gate_zero is None else xs + gate_zero
                x1 = _layer_norm(alpha * xs + g2 * y0[rows, :], lg0, lb0)
                x1_ref[in_tile(sub, rows), :] = x1
                h_ref[sub, rows, :] = (x1 * sc1 + sh).astype(BF)
        else:
            def step(rows, gate_zero=None):
                xs = x_ref[in_tile(sub, rows), :]
                xs = xs if gate_zero is None else xs + gate_zero
                h_ref[sub, rows, :] = (xs * sc1 + sh).astype(BF)
        return [functools.partial(step, rows) for rows in groups]

    def up_chunks(sub):
        def step(c):
            gate = _dot(h_ref[sub], wi_ref[:, c * ck:(c + 1) * ck])
            up = _dot(h_ref[sub], wi_ref[:, dff + c * ck:dff + (c + 1) * ck])
            acc_ref[sub, :, c * ck:(c + 1) * ck] = (_silu(gate) * up).astype(BF)
            bits = pltpu.bitcast(gate[0:ROW_GROUP, 0:LANES], jnp.uint32)
            half_word = jnp.uint32(16)
            zero = pltpu.bitcast(
                lax.shift_right_logical(lax.shift_right_logical(bits, half_word), half_word), F32)
            return jnp.concatenate([zero] * (d // LANES), axis=1)
        return [functools.partial(step, c) for c in range(dff // ck)]

    def epilogue(sub):
        def step(rows, gate_zero=None):
            y = y_ref[sub, rows, :]
            if gate_zero is not None:
                y = y + gate_zero
            o_ref[in_tile(sub, rows), :] = _layer_norm(
                alpha * xin_ref[in_tile(sub, rows), :] + gt_half * y, lg, lb)
        return [functools.partial(step, rows) for rows in groups]

    def run(main, filler=()):
        filler = list(filler)
        per = -(-len(filler) // max(len(main), 1))
        for step in main:
            token = step()
            for _ in range(per):
                if filler:
                    filler.pop(0)(token)
        for f in filler:
            f()

    run(prologue(0))
    pending = []
    for sub in range(n_sub):
        nxt = prologue(sub + 1) if sub + 1 < n_sub else []
        run(up_chunks(sub), pending + nxt)
        y_ref[sub] = _dot(acc_ref[sub], wo_ref[...])
        pending = epilogue(sub)
    run(pending)


def _ffn_sublayer(x, mod, w_in, w_out, ln_g, ln_b, layer, *, mod_row, ln_row, alpha, mix=None):
    b, s, d = x.shape
    dff = w_out.shape[1]
    ck = FFN_CHUNK if dff % FFN_CHUNK == 0 else dff
    tm = min(FFN_ROWS, s)
    n_sub = FFN_SUBTILES if tm % (FFN_SUBTILES * ROW_GROUP) == 0 else 1
    tok = lambda n: pl.BlockSpec((None, tm, n), lambda bi, i: (bi, i, 0))
    lay = lambda bi, i: (layer, 0, 0)
    in_specs = [tok(d), pl.BlockSpec((None, N_MOD, d), lambda bi, i: (bi, 0, 0))]
    args = [x, mod]
    if mix is not None:
        oa, ob, oc, od, w_mix = mix
        gw = GROUP_WIDTH
        in_specs += [tok(gw), tok(gw), tok(gw), tok(gw), _resident((None, d, d), lay)]
        args += [oa, ob, oc, od, w_mix]
    in_specs += [_resident((None, d, 2 * dff), lay), _resident((None, dff, d), lay),
                 _resident((None,) + ln_g.shape[1:], lay), _resident((None,) + ln_b.shape[1:], lay)]
    args += [w_in, w_out, ln_g, ln_b]
    return pl.pallas_call(
        functools.partial(_ffn_kernel, mod_row=mod_row, ln_row=ln_row, dff=dff, ck=ck, alpha=alpha,
                          with_mix=mix is not None),
        grid=(b, s // tm),
        in_specs=in_specs,
        out_specs=tok(d),
        out_shape=jax.ShapeDtypeStruct((b, s, d), F32),
        scratch_shapes=[pltpu.VMEM((n_sub, tm // n_sub, dff), BF), pltpu.VMEM((n_sub, tm // n_sub, d), BF),
                        pltpu.VMEM((n_sub, tm // n_sub, d), F32)]
        + ([pltpu.VMEM((tm, d), F32)] if mix is not None else []),
        compiler_params=_params(2),
        name="mix_out_ffn" if mix is not None else "ffn_sublayer",
    )(*args)


_C_CQ = 0
_C_CKV = 256
_C_KRF = 384
_C_DU = 512
_C_DV = 768
_C_FOX = 1024
_C_HG = 1792
_C_END = 2816
CUM_ROWS = 256


def _mix_in_kernel(x_ref, mod_ref, w_ref, wuq_ref, wukv_ref, qg_ref, kvg_ref, ct_ref, st_ref,
                   fb_ref, dlg_ref, dlb_ref, ws_ref, bs_ref,
                   hg_ref, mq_ref, mk_ref, mv_ref, fq_ref, fk_ref, fv_ref, fft_ref, od_ref,
                   carry_ref, h_ref, *, tm, mla_scale, fox_scale):
    d = x_ref.shape[1]
    sh = jnp.broadcast_to(mod_ref[3:4, :], (ROW_GROUP, d))
    sc1 = jnp.broadcast_to(1.0 + mod_ref[4:5, :], (ROW_GROUP, d))
    for r in range(0, tm, ROW_GROUP):
        h_ref[r:r + ROW_GROUP, :] = (x_ref[r:r + ROW_GROUP, :] * sc1 + sh).astype(BF)
    h = h_ref[...]
    gw = GROUP_WIDTH

    @pl.when(pl.program_id(1) == 0)
    def _():
        carry_ref[...] = jnp.zeros_like(carry_ref)

    s1 = _dot(h, w_ref[:, _C_CQ:_C_FOX])
    c_q = s1[:, _C_CQ:_C_CKV]
    c_kv = s1[:, _C_CKV:_C_KRF]
    krf = s1[:, _C_KRF:_C_DU]
    d_u = s1[:, _C_DU:_C_DV]
    d_v = s1[:, _C_DV:_C_FOX]

    fox = _dot(h, w_ref[:, _C_FOX:_C_HG])
    fq_ref[...] = (fox[:, 0:gw] * fox_scale).astype(BF)
    fk_ref[...] = fox[:, gw:2 * gw].T.astype(BF)
    fv_ref[...] = fox[:, 2 * gw:3 * gw].astype(BF)
    nq = _rms_norm(c_q, qg_ref[...]).astype(BF)
    nkv = _rms_norm(c_kv, kvg_ref[...]).astype(BF)
    logf = _log_sigmoid(krf + fb_ref[...])
    u = _gelu_tanh(d_u)
    vn = _layer_norm(_gelu_tanh(d_v), dlg_ref[...], dlb_ref[...])

    q2 = _dot(nq, wuq_ref[...])
    kv2 = _dot(nkv, wukv_ref[...])
    cr = min(CUM_ROWS, tm)
    tri = (_iota2((cr, cr), 0) >= _iota2((cr, cr), 1)).astype(BF)
    pick = (_iota2((SUBLANES, LANES), 0) == _iota2((SUBLANES, LANES), 1)).astype(BF)
    run = carry_ref[...]
    for j in range(tm // cr):
        f_hi, f_mid, f_lo = _split3(logf[j * cr:(j + 1) * cr, :])
        cum = _dot(tri, f_hi) + _dot(tri, f_mid) + _dot(tri, f_lo) + run
        run = cum[cr - 1:cr, :]
        c_hi, c_mid, c_lo = _split3(cum)
        fft_ref[:, j * cr:(j + 1) * cr] = _dot_nt(pick, c_hi) + _dot_nt(pick, c_mid) + _dot_nt(pick, c_lo)
    carry_ref[...] = run
    cc = GMLP_CHUNK
    causal = _iota2((cc, cc), 0) >= _iota2((cc, cc), 1)
    lane_group = _iota2((cc, gw), 1) >> HEAD_SHIFT
    w_cat = jnp.concatenate([jnp.where(causal, ws_ref[g], 0.0) for g in range(N_HEADS)],
                            axis=1).astype(BF)
    for j in range(tm // cc):
        vj = vn[j * cc:(j + 1) * cc, :]
        v_stack = jnp.concatenate([jnp.where(lane_group == g, vj, 0.0) for g in range(N_HEADS)],
                                  axis=0).astype(BF)
        mixed = bs_ref[...] + _dot(w_cat, v_stack)
        od_ref[j * cc:(j + 1) * cc, :] = (u[j * cc:(j + 1) * cc, :] * mixed).astype(BF)

    hg_ref[...] = _dot(h, w_ref[:, _C_HG:_C_END])
    ct = ct_ref[...]
    st = st_ref[...]
    ct4 = jnp.concatenate([ct] * N_HEADS, axis=1)
    st4 = jnp.concatenate([st] * N_HEADS, axis=1)
    nslot = N_HEADS * HEAD_SLOT
    q_rot = q2[:, :nslot] * ct4 + q2[:, nslot:] * st4
    mq_ref[...] = (q_rot * mla_scale).astype(BF)
    half = MLA_ROPE // 2
    lane = _iota2((tm, HEAD_SLOT), 1)
    kr = jnp.where(lane >= MLA_NOPE, krf, 0.0)
    kr_swap = jnp.where(lane < MLA_NOPE + half, pltpu.roll(kr, HEAD_SLOT - half, 1), pltpu.roll(kr, half, 1))
    kr_rot = kr * ct + kr_swap * st
    mk_ref[...] = (kv2[:, :nslot] + jnp.concatenate([kr_rot] * N_HEADS, axis=1)).T.astype(BF)
    mv_ref[...] = kv2[:, nslot:].astype(BF)


def _place_heads(w, width, offsets):
    cols = []
    for hd in range(N_HEADS):
        blk = w[:, hd * width:(hd + 1) * width]
        off = offsets[hd]
        cols.append(jnp.pad(blk, ((0, 0), (off, HEAD_SLOT - off - width))))
    return jnp.concatenate(cols, axis=1)


def _mix_in_weights(mix_w_in, mla_w_uq, mla_w_ukv):
    d = mix_w_in.shape[0]
    gw = GROUP_WIDTH
    w = mix_w_in.astype(BF)
    n_hg = 4 * gw
    n_a = n_hg + MLA_Q_LORA + MLA_KV_LORA
    b_kr = w[:, n_a:n_a + MLA_ROPE]
    o = n_a + MLA_ROPE
    c_qkv = w[:, o:o + 3 * gw]
    c_f = w[:, o + 3 * gw:o + 3 * gw + N_HEADS]
    d_uv = w[:, o + 3 * gw + N_HEADS:]
    z = lambda n: jnp.zeros((d, n), BF)
    w_big = jnp.concatenate([
        w[:, n_hg:n_a], c_f, z(MLA_NOPE - N_HEADS), b_kr, z(HEAD_SLOT - MLA_NOPE - MLA_ROPE),
        d_uv, c_qkv, w[:, :n_hg]], axis=1)
    assert w_big.shape[1] == _C_END
    half = MLA_ROPE // 2
    qk = MLA_NOPE + MLA_ROPE
    uq = mla_w_uq.reshape(MLA_Q_LORA, N_HEADS, qk)
    zq = lambda n: jnp.zeros((MLA_Q_LORA, N_HEADS, n), mla_w_uq.dtype)
    uq_n = jnp.concatenate([uq, zq(HEAD_SLOT - qk)], axis=2)
    uq_s = jnp.concatenate([zq(MLA_NOPE), uq[:, :, MLA_NOPE + half:], uq[:, :, MLA_NOPE:MLA_NOPE + half],
                            zq(HEAD_SLOT - qk)], axis=2)
    w_uq2 = jnp.concatenate([uq_n.reshape(MLA_Q_LORA, -1), uq_s.reshape(MLA_Q_LORA, -1)], axis=1).astype(BF)
    ukv = mla_w_ukv.reshape(MLA_KV_LORA, N_HEADS, MLA_NOPE + HEAD_DIM)
    zero_off = (0,) * N_HEADS
    pair_off = tuple((hd % 2) * HEAD_DIM for hd in range(N_HEADS))
    k_n = _place_heads(ukv[:, :, :MLA_NOPE].reshape(MLA_KV_LORA, -1), MLA_NOPE, zero_off)
    v_p = _place_heads(ukv[:, :, MLA_NOPE:].reshape(MLA_KV_LORA, -1), HEAD_DIM, pair_off)
    w_ukv2 = jnp.concatenate([k_n, v_p], axis=1).astype(BF)
    return w_big, w_uq2, w_ukv2


def _rope_tables(s):
    half = MLA_ROPE // 2
    inv_freq = ROPE_THETA ** (-jnp.arange(half, dtype=F32) / half)
    ang = jnp.arange(s, dtype=F32)[:, None] * inv_freq[None, :]
    cos, sin = jnp.cos(ang), jnp.sin(ang)
    tail = HEAD_SLOT - MLA_NOPE - MLA_ROPE
    ct = jnp.concatenate([jnp.ones((s, MLA_NOPE), F32), cos, cos, jnp.zeros((s, tail), F32)], axis=1)
    st = jnp.concatenate([jnp.zeros((s, MLA_NOPE), F32), -sin, sin, jnp.zeros((s, tail), F32)], axis=1)
    return ct, st


def _mix_in(x, mod, w_big, w_uq2, w_ukv2, q_norm_g, kv_norm_g, ct, st, fox_b_f,
            gmlp_ln_g, gmlp_ln_b, gmlp_w_s, gmlp_b_s):
    b, s, d = x.shape
    tm = min(MIX_ROWS, s)
    nslot = N_HEADS * HEAD_SLOT
    gw = GROUP_WIDTH
    fb = jnp.pad(fox_b_f.reshape(1, N_HEADS), ((0, 0), (0, LANES - N_HEADS)))
    bias_tile = jnp.repeat(gmlp_b_s.T, HEAD_DIM, axis=1)
    full2 = lambda bi, i: (0, 0)
    tok = lambda n: pl.BlockSpec((None, tm, n), lambda bi, i: (bi, i, 0))
    shp = lambda n, dt: jax.ShapeDtypeStruct((b, s, n), dt)
    tok_t = lambda n: pl.BlockSpec((None, n, tm), lambda bi, i: (bi, 0, i))
    shp_t = lambda n: jax.ShapeDtypeStruct((b, n, s), BF)
    return pl.pallas_call(
        functools.partial(_mix_in_kernel, tm=tm, mla_scale=float((MLA_NOPE + MLA_ROPE) ** -0.5 * LOG2_E),
                          fox_scale=float(HEAD_DIM ** -0.5 * LOG2_E)),
        grid=(b, s // tm),
        in_specs=[
            tok(d),
            pl.BlockSpec((None, N_MOD, d), lambda bi, i: (bi, 0, 0)),
            _resident(w_big.shape, full2),
            _resident(w_uq2.shape, full2),
            _resident(w_ukv2.shape, full2),
            _resident((1, MLA_Q_LORA), full2),
            _resident((1, MLA_KV_LORA), full2),
            pl.BlockSpec((tm, HEAD_SLOT), lambda bi, i: (i, 0)),
            pl.BlockSpec((tm, HEAD_SLOT), lambda bi, i: (i, 0)),
            _resident((1, LANES), full2),
            _resident((1, gw), full2),
            _resident((1, gw), full2),
            _resident(gmlp_w_s.shape, lambda bi, i: (0, 0, 0)),
            _resident(bias_tile.shape, full2),
        ],
        out_specs=[tok(4 * gw), tok(nslot), tok_t(nslot), tok(nslot), tok(gw), tok_t(gw), tok(gw),
                   pl.BlockSpec((None, SUBLANES, tm), lambda bi, i: (bi, 0, i)), tok(gw)],
        out_shape=[shp(4 * gw, F32), shp(nslot, BF), shp_t(nslot), shp(nslot, BF),
                   shp(gw, BF), shp_t(gw), shp(gw, BF),
                   jax.ShapeDtypeStruct((b, SUBLANES, s), F32), shp(gw, BF)],
        scratch_shapes=[pltpu.VMEM((1, LANES), F32), pltpu.VMEM((tm, d), BF)],
        compiler_params=_params(2),
        name="mixer_in",
    )(x, mod, w_big, w_uq2, w_ukv2, q_norm_g.reshape(1, -1), kv_norm_g.reshape(1, -1), ct, st, fb,
      gmlp_ln_g.reshape(1, gw), gmlp_ln_b.reshape(1, gw), gmlp_w_s, bias_tile)


def _hgrn_kernel(hg_ref, lbl_ref, ng_ref, o_ref, st_ref, oacc_ref, gk_ref, kv_ref, stb_ref,
                 *, layer, depth, tb):
    gw = GROUP_WIDTH
    ch = HGRN_CHUNK
    half = ch // 2
    n_ch = tb // ch

    @pl.when(pl.program_id(1) == 0)
    def _():
        st_ref[...] = jnp.zeros_like(st_ref)

    rows = [lbl_ref[j:j + 1, :] for j in range(depth)]
    top = functools.reduce(jnp.maximum, rows)
    ex = [jnp.exp(r - top) for r in rows]
    sm = [e / sum(ex) for e in ex]
    lb = sum(sm[:layer + 1]) - sm[0]
    lb_floor = jnp.maximum(lb, LB_FLOOR)

    z_q = hg_ref[:, 0:gw]
    z_f = hg_ref[:, gw:2 * gw]
    val = hg_ref[:, 2 * gw:3 * gw]
    z_g = hg_ref[:, 3 * gw:4 * gw]
    log_f = jnp.log(lb_floor + (1.0 - lb) * _sigmoid(z_f))
    kk = (1.0 - lb) * _sigmoid(-z_f) - (lb_floor - lb)
    qf = _silu(z_q)

    r_i = _iota2((tb, tb), 0)
    c_i = _iota2((tb, tb), 1)
    same_chunk = (r_i >> CHUNK_SHIFT) == (c_i >> CHUNK_SHIFT)
    tri_blk = (same_chunk & (r_i >= c_i)).astype(BF)
    ones_blk = same_chunk.astype(BF)
    hi, mid, lo = _split3(log_f)
    g_cum = _dot(tri_blk, hi) + _dot(tri_blk, mid) + _dot(tri_blk, lo)
    g_last = _dot(ones_blk, hi) + _dot(ones_blk, mid) + _dot(ones_blk, lo)
    q_dec = qf * jnp.exp(g_cum)
    k_end = kk * jnp.exp(g_last - g_cum)
    dec = jnp.exp(g_last)

    head_ones = ((_iota2((gw, gw), 0) >> HEAD_SHIFT) == (_iota2((gw, gw), 1) >> HEAD_SHIFT)).astype(BF)
    lane_head = _iota2((ch, gw), 1) >> HEAD_SHIFT
    head_lanes = [(lane_head == hd).astype(F32) for hd in range(N_HEADS)]
    pair_lanes = [(lane_head >> 1 == j).astype(F32) for j in range(2)]
    low_half = (_iota2((ch, LANES), 1) < HEAD_DIM).astype(F32)
    high_half = 1.0 - low_half

    g2 = g_cum * LOG2_E
    gk_ref[0] = g2
    gk_ref[1] = kk
    t_half = _iota2((half, gw), 0)
    causal_half = [t_half >= s_ for s_ in range(half)]

    def row(ref_row):
        return jnp.broadcast_to(ref_row, (half, gw))

    for c in range(n_ch):
        r0 = c * ch
        v_c = val[r0:r0 + ch, :]
        v_lo, v_hi = v_c[:, :LANES], v_c[:, LANES:]
        v_stack = jnp.concatenate([v_lo * low_half, v_lo * high_half, v_hi * low_half, v_hi * high_half],
                                  axis=0).astype(BF)
        ke_c = k_end[r0:r0 + ch, :]
        k_stack = jnp.concatenate([ke_c * head_lanes[hd] for hd in range(N_HEADS)],
                                  axis=0).astype(BF)
        kv_ref[c] = _dot_tn(v_stack, k_stack)
    for c in range(n_ch):
        st_c = st_ref[...]
        stb_ref[c] = st_c.astype(BF)
        st_ref[...] = st_c * dec[c * ch:c * ch + 1, :] + kv_ref[c]
    for c in range(n_ch):
        r0 = c * ch
        qd_c = q_dec[r0:r0 + ch, :]
        q_pairs = jnp.concatenate([qd_c * pair_lanes[0], qd_c * pair_lanes[1]], axis=0).astype(BF)
        inter = _dot_nt(q_pairs, stb_ref[c])
        oacc_ref[r0:r0 + ch, :] = jnp.concatenate([inter[:ch, :], inter[ch:, :]], axis=1)
    for c in range(n_ch):
        r0 = c * ch
        g_a, g_b = g2[r0:r0 + half, :], g2[r0 + half:r0 + ch, :]
        q_a, q_b = qf[r0:r0 + half, :], qf[r0 + half:r0 + ch, :]
        early, late = [], []
        for s_ in range(ch):
            gs = row(gk_ref[0, r0 + s_:r0 + s_ + 1, :])
            ks = row(gk_ref[1, r0 + s_:r0 + s_ + 1, :])
            if s_ < half:
                pa = jnp.where(causal_half[s_], jnp.exp2(g_a - gs), 0.0) * (q_a * ks)
                pb = jnp.exp2(g_b - gs) * (q_b * ks)
                early += [pa, pb]
            else:
                late.append(jnp.where(causal_half[s_ - half], jnp.exp2(g_b - gs), 0.0) * (q_b * ks))
        p_rows = jnp.concatenate(early + late, axis=0).astype(BF)
        scores = _dot(p_rows, head_ones)
        o_a = o_b = None
        for s_ in range(ch):
            vs = row(hg_ref[r0 + s_:r0 + s_ + 1, 2 * gw:3 * gw])
            if s_ < half:
                ta = scores[s_ * ch:s_ * ch + half, :] * vs
                tb_ = scores[s_ * ch + half:(s_ + 1) * ch, :] * vs
                o_a = ta if o_a is None else o_a + ta
            else:
                base = half * ch + (s_ - half) * half
                tb_ = scores[base:base + half, :] * vs
            o_b = tb_ if o_b is None else o_b + tb_
        oacc_ref[r0:r0 + half, :] += o_a
        oacc_ref[r0 + half:r0 + ch, :] += o_b

    o = oacc_ref[...]
    h2, l2, _ = _split3(o * o)
    ms = (_dot(h2, head_ones) + _dot(l2, head_ones)) * (1.0 / HEAD_DIM)
    o = o * lax.rsqrt(ms + RMS_EPS) * ng_ref[...]
    o_ref[...] = (o * _silu(z_g)).astype(BF)


def _hgrn(hg, lb_logits, norm_g, *, layer):
    b, s, _ = hg.shape
    depth = lb_logits.shape[0]
    gw = GROUP_WIDTH
    tb = min(HGRN_ROWS, s)
    return pl.pallas_call(
        functools.partial(_hgrn_kernel, layer=layer, depth=depth, tb=tb),
        grid=(b, s // tb),
        in_specs=[
            pl.BlockSpec((None, tb, 4 * gw), lambda bi, i: (bi, i, 0)),
            pl.BlockSpec((depth, gw), lambda bi, i: (0, 0)),
            pl.BlockSpec((1, gw), lambda bi, i: (0, 0)),
        ],
        out_specs=pl.BlockSpec((None, tb, gw), lambda bi, i: (bi, i, 0)),
        out_shape=jax.ShapeDtypeStruct((b, s, gw), BF),
        scratch_shapes=[pltpu.VMEM((2 * HEAD_DIM, gw), F32),
                        pltpu.VMEM((tb, gw), F32),
                        pltpu.VMEM((2, tb, gw), F32),
                        pltpu.VMEM((tb // HGRN_CHUNK, 2 * HEAD_DIM, gw), F32),
                        pltpu.VMEM((tb // HGRN_CHUNK, 2 * HEAD_DIM, gw), BF)],
        compiler_params=_params(2),
        name="hgrn2",
    )(hg, lb_logits, norm_g.reshape(1, gw))


def _attn_kernel(*refs, s_len, tq, has_bias, packed):
    if has_bias:
        q_ref, k_ref, v_ref, bk_ref, o_ref, s_ref = refs
    else:
        q_ref, k_ref, v_ref, o_ref, s_ref = refs
    pair = pl.program_id(1)
    n_q = s_len // tq
    n_g = tq // LANES
    causal = _iota2((tq, tq), 0) >= _iota2((tq, tq), 1)
    lane = _iota2((tq, LANES), 1)
    own_half = [(lane >> HEAD_SHIFT) == hh for hh in range(2)]
    keep = [m.astype(F32).astype(BF) for m in own_half]
    sum_lane = [HEAD_DIM, 0]
    ones_col = [(lane == sum_lane[hh]).astype(F32).astype(BF) for hh in range(2)]
    def head_lanes(hh):
        return slice(0, LANES) if packed else slice(hh * HEAD_SLOT, (hh + 1) * HEAD_SLOT)

    def pass1(qi, hh):
        lanes = head_lanes(hh)
        stage = s_ref.at[2 * (qi % 2) + hh]
        q = q_ref[qi * tq:(qi + 1) * tq, lanes]
        if packed:
            q = q * keep[hh]
        m_part = None
        for j in range(qi + 1):
            cols = slice(j * tq, (j + 1) * tq)
            sj = _dot(q, k_ref[lanes, cols])
            if has_bias:
                head = 2 * pair + hh
                row_id = _iota2((SUBLANES, tq), 0)
                f_s = jnp.sum(jnp.where(row_id == head, bk_ref[:, cols], 0.0), axis=0, keepdims=True)
                sj = sj - f_s * LOG2_E
            if j == qi:
                sj = jnp.where(causal, sj, NEG_BIG)
            stage[:, cols] = sj
            for g in range(n_g):
                blk = sj[:, g * LANES:(g + 1) * LANES]
                m_part = blk if m_part is None else jnp.maximum(m_part, blk)
        return jnp.broadcast_to(jnp.max(m_part, axis=1, keepdims=True), (tq, LANES))

    def pass2(qi, hh, m_b):
        lanes = head_lanes(hh)
        stage = s_ref.at[2 * (qi % 2) + hh]
        acc = jnp.zeros((tq, LANES), F32)
        for j in range(qi + 1):
            ps = [jnp.exp2(stage[:, j * tq + g * LANES:j * tq + (g + 1) * LANES] - m_b).astype(BF)
                  for g in range(n_g)]
            vj = v_ref[j * tq:(j + 1) * tq, lanes]
            vj = (vj * keep[hh] if packed else vj) + ones_col[hh]
            acc = acc + _dot(jnp.concatenate(ps, axis=1), vj)
        row_sum = acc[:, sum_lane[hh]:sum_lane[hh] + 1]
        return jnp.where(own_half[hh], acc, 0.0) * (1.0 / row_sum)

    units = [(qi, hh) for qi in range(n_q) for hh in range(2)]
    ahead = 3
    maxes = {u: pass1(*u) for u in units[:ahead]}
    outs = {}
    for i, unit in enumerate(units):
        if i + ahead < len(units):
            maxes[units[i + ahead]] = pass1(*units[i + ahead])
        outs[unit] = pass2(*unit, maxes.pop(unit))
        qi, hh = unit
        if hh == 1:
            o_ref[qi * tq:(qi + 1) * tq, :] = (outs.pop((qi, 0)) + outs.pop((qi, 1))).astype(BF)


def _attention(q, k, v, bias_k=None, *, packed):
    b, s, _ = q.shape
    tq = min(ATT_Q_ROWS, s)
    has_bias = bias_k is not None
    pairw = LANES if packed else 2 * HEAD_SLOT
    seq = lambda n: pl.BlockSpec((None, s, n), lambda bi, p: (bi, 0, p))
    in_specs = [seq(pairw), pl.BlockSpec((None, pairw, s), lambda bi, p: (bi, p, 0)), seq(pairw)]
    args = [q, k, v]
    if has_bias:
        in_specs += [pl.BlockSpec((None, SUBLANES, s), lambda bi, p: (bi, 0, 0))]
        args += [bias_k]
    return pl.pallas_call(
        functools.partial(_attn_kernel, s_len=s, tq=tq, has_bias=has_bias, packed=packed),
        grid=(b, N_HEADS // 2),
        in_specs=in_specs,
        out_specs=seq(LANES),
        out_shape=jax.ShapeDtypeStruct((b, s, (N_HEADS // 2) * LANES), BF),
        scratch_shapes=[pltpu.VMEM((4, tq, s), F32)],
        compiler_params=_params(2),
        name="fox_attention" if has_bias else "mla_attention",
    )(*args)


def kernel(x, c, ada_w, ada_b, ln_g, ln_b, ffn1_w_in, ffn1_w_out, ffn2_w_in, ffn2_w_out, mix_w_in, mix_w_out, hgrn_lb_logits, hgrn_norm_g, mla_q_norm_g, mla_kv_norm_g, mla_w_uq, mla_w_ukv, fox_b_f, gmlp_ln_g, gmlp_ln_b, gmlp_w_s, gmlp_b_s):
    depth = ada_w.shape[0]
    b, s, d = x.shape
    assert d == N_HEADS * GROUP_WIDTH and s % GMLP_CHUNK == 0
    alpha = float((2 * depth) ** 0.25)
    mod_all = _modulation(c, ada_w, ada_b).reshape(depth, b, N_MOD, d)
    ct, st = _rope_tables(s)
    w1_in, w1_out = ffn1_w_in.astype(BF), ffn1_w_out.astype(BF)
    w2_in, w2_out = ffn2_w_in.astype(BF), ffn2_w_out.astype(BF)
    w_mix_out = mix_w_out.astype(BF)
    for l in range(depth):
        mod = mod_all[l]
        x = _ffn_sublayer(x, mod, w1_in, w1_out, ln_g, ln_b, l, mod_row=0, ln_row=0, alpha=alpha)
        w_big, w_uq2, w_ukv2 = _mix_in_weights(mix_w_in[l], mla_w_uq[l], mla_w_ukv[l])
        hg, mq, mk, mv, fq, fk, fv, fft, od = _mix_in(
            x, mod, w_big, w_uq2, w_ukv2, mla_q_norm_g[l], mla_kv_norm_g[l], ct, st, fox_b_f[l],
            gmlp_ln_g[l], gmlp_ln_b[l], gmlp_w_s[l], gmlp_b_s[l])
        oa = _hgrn(hg, hgrn_lb_logits, hgrn_norm_g[l], layer=l)
        ob = _attention(mq, mk, mv, packed=False)
        oc = _attention(fq, fk, fv, fft, packed=True)
        x = _ffn_sublayer(x, mod, w2_in, w2_out, ln_g, ln_b, l, mod_row=6, ln_row=2, alpha=alpha,
                          mix=(oa, ob, oc, od, w_mix_out))
    return x
```
